```python
import math
import jax, jax.numpy as jnp
from jax import lax
import numpy as np

D_MODEL = 4096
BATCH = 4
SEQ = 2048
DEPTH = 4
DEC_BATCH = 128
DEC_SEQ = 1
PAST_LEN = 16384
PAGE_SIZE = 128

N_MIXERS = 2
GDN_HEADS = 32
GDN_DK = 128
GDN_DV = 128
GDN_KEY_DIM = GDN_HEADS * GDN_DK
GDN_VAL_DIM = GDN_HEADS * GDN_DV
GDN_CONV_DIM = 2 * GDN_KEY_DIM + GDN_VAL_DIM
GDN_CHUNK = 64
SHORT_CONV = 4
D_RNN = D_MODEL
LRU_BLOCKS = 16
LRU_BW = D_RNN // LRU_BLOCKS
LRU_C = 8.0
D_FF = 11008
FFN_CONV = 3
N_GDN = (DEPTH + 1) // 2
N_LRU = DEPTH // 2
NORM_EPS = 1e-6
L2_EPS = 1e-6

kernel_name = "hybrid_gdn_rglru_convglu_step"


def rmsnorm(x, w):
    x32 = x.astype(jnp.float32)
    y = x32 * lax.rsqrt(jnp.mean(x32 * x32, axis=-1, keepdims=True) + NORM_EPS)
    return (y * w.astype(jnp.float32)).astype(x.dtype)


def l2norm(x):
    return x * lax.rsqrt(jnp.sum(x * x, axis=-1, keepdims=True) + L2_EPS)


def causal_conv(x, buf, w, b=None):
    K = w.shape[0]
    T = x.shape[1]
    xp = jnp.concatenate([buf.astype(x.dtype), x], axis=1)
    y = xp[:, 0:T] * w[0]
    for k in range(1, K):
        y = y + xp[:, k:k + T] * w[k]
    if b is not None:
        y = y + b
    return y, xp[:, -(K - 1):]


def gdn_chunked(q, k, v, g, beta, S0):
    B, T, H, DK = q.shape
    DV = v.shape[-1]
    C = GDN_CHUNK
    N = T // C
    ch = lambda t: t.reshape(B, N, C, H, -1).transpose(0, 1, 3, 2, 4)
    qc, kc, vc = ch(q), ch(k), ch(v)
    gc = jnp.cumsum(g.reshape(B, N, C, H).transpose(0, 1, 3, 2), axis=-1)
    bc = beta.reshape(B, N, C, H).transpose(0, 1, 3, 2)
    causal = jnp.tril(jnp.ones((C, C), dtype=bool))
    strict = jnp.tril(jnp.ones((C, C), dtype=bool), -1)
    diff = gc[..., :, None] - gc[..., None, :]
    L = jnp.where(causal, jnp.exp(jnp.where(causal, diff, 0.0)), 0.0)
    kb = kc * bc[..., None]
    vb = vc * bc[..., None]
    A = jnp.where(strict, jnp.einsum('bnhik,bnhjk->bnhij', kb, kc) * L, 0.0)
    eye = jnp.broadcast_to(jnp.eye(C, dtype=A.dtype), A.shape)
    Tm = lax.linalg.triangular_solve(eye + A, eye, left_side=True, lower=True)
    u = jnp.einsum('bnhij,bnhjv->bnhiv', Tm, vb)
    w = jnp.einsum('bnhij,bnhjk->bnhik', Tm, kb * jnp.exp(gc)[..., None])
    qk = jnp.einsum('bnhik,bnhjk->bnhij', qc, kc) * L
    dl = gc[..., -1]
    k_tail = kc * jnp.exp(dl[..., None] - gc)[..., None]
    q_dec = qc * jnp.exp(gc)[..., None]

    def step(S, xs):
        u_n, w_n, qd_n, qk_n, kt_n, dl_n = xs
        v_new = u_n - jnp.einsum('bhck,bhkv->bhcv', w_n, S)
        o_n = jnp.einsum('bhck,bhkv->bhcv', qd_n, S) + jnp.einsum('bhcj,bhjv->bhcv', qk_n, v_new)
        S = S * jnp.exp(dl_n)[..., None, None] + jnp.einsum('bhck,bhcv->bhkv', kt_n, v_new)
        return S, o_n

    xs = tuple(jnp.moveaxis(t, 1, 0) for t in (u, w, q_dec, qk, k_tail, dl))
    S, o = lax.scan(step, S0, xs)
    o = o.transpose(1, 0, 3, 2, 4).reshape(B, T, H, DV)
    return o, S


def gdn_recurrent(q, k, v, g, beta, S0):
    def step(S, xs):
        q_t, k_t, v_t, g_t, b_t = xs
        S = S * jnp.exp(g_t)[..., None, None]
        kS = jnp.einsum('bhk,bhkv->bhv', k_t, S)
        S = S + jnp.einsum('bhk,bhv->bhkv', k_t, b_t[..., None] * (v_t - kS))
        return S, jnp.einsum('bhk,bhkv->bhv', q_t, S)

    xs = tuple(jnp.swapaxes(t, 0, 1) for t in (q, k, v, g, beta))
    S, o = lax.scan(step, S0, xs)
    return jnp.swapaxes(o, 0, 1), S


def gdn_mixer(h, S0, conv_buf, w_in, conv_w, A_log, dt_bias, norm_w, w_out, chunked):
    B, T, _ = h.shape
    H = GDN_HEADS
    proj = h @ w_in
    qkv = proj[..., :GDN_CONV_DIM]
    z = proj[..., GDN_CONV_DIM:GDN_CONV_DIM + GDN_VAL_DIM]
    a_raw = proj[..., GDN_CONV_DIM + GDN_VAL_DIM:GDN_CONV_DIM + GDN_VAL_DIM + H].astype(jnp.float32)
    b_raw = proj[..., GDN_CONV_DIM + GDN_VAL_DIM + H:].astype(jnp.float32)
    qkv, new_buf = causal_conv(qkv, conv_buf, conv_w)
    qkv = jax.nn.silu(qkv).astype(jnp.float32)
    q = qkv[..., :GDN_KEY_DIM].reshape(B, T, H, GDN_DK)
    k = qkv[..., GDN_KEY_DIM:2 * GDN_KEY_DIM].reshape(B, T, H, GDN_DK)
    v = qkv[..., 2 * GDN_KEY_DIM:].reshape(B, T, H, GDN_DV)
    q = l2norm(q) * (GDN_DK ** -0.5)
    k = l2norm(k)
    g = -jnp.exp(A_log.astype(jnp.float32)) * jax.nn.softplus(a_raw + dt_bias.astype(jnp.float32))
    beta = jax.nn.sigmoid(b_raw)
    S0 = S0.astype(jnp.float32)
    if chunked:
        o, S = gdn_chunked(q, k, v, g, beta, S0)
    else:
        o, S = gdn_recurrent(q, k, v, g, beta, S0)
    o = o * lax.rsqrt(jnp.mean(o * o, axis=-1, keepdims=True) + NORM_EPS) * norm_w.astype(jnp.float32)
    o = o * jax.nn.silu(z.astype(jnp.float32).reshape(B, T, H, GDN_DV))
    out = o.reshape(B, T, GDN_VAL_DIM).astype(h.dtype) @ w_out
    return out, S.astype(h.dtype), new_buf


def lru_scan(a, b, h0):
    b = b.at[:, 0].add(a[:, 0] * h0)

    def combine(l, r):
        return (l[0] * r[0], r[0] * l[1] + r[1])

    _, hs = lax.associative_scan(combine, (a, b), axis=1)
    return hs


def rglru_mixer(h, h0, conv_buf, w_in, b_in, conv_w, conv_b, w_gate, b_gate, lam, w_out):
    B, T, _ = h.shape
    proj = h @ w_in + b_in
    xb, yb = proj[..., :D_RNN], proj[..., D_RNN:]
    y_gate = jax.nn.gelu(yb, approximate=True)
    xc, new_buf = causal_conv(xb, conv_buf, conv_w, conv_b)
    xc32 = xc.astype(jnp.float32)
    gates = jnp.einsum('btnc,ncd->btnd', xc32.reshape(B, T, LRU_BLOCKS, LRU_BW), w_gate.astype(jnp.float32))
    bg = b_gate.astype(jnp.float32)
    r = jax.nn.sigmoid(gates[..., :LRU_BW].reshape(B, T, D_RNN) + bg[:D_RNN])
    i = jax.nn.sigmoid(gates[..., LRU_BW:].reshape(B, T, D_RNN) + bg[D_RNN:])
    log_a = -LRU_C * r * jax.nn.softplus(-lam.astype(jnp.float32))
    a = jnp.exp(log_a)
    mult = jnp.sqrt(-jnp.expm1(2.0 * log_a))
    hs = lru_scan(a, mult * (i * xc32), h0.astype(jnp.float32))
    out = (hs.astype(h.dtype) * y_gate) @ w_out
    return out, hs[:, -1].astype(h.dtype), new_buf


def conv_ffn(h, buf, w_in, conv_w, conv_b, w_out):
    proj = h @ w_in
    gate, val = proj[..., :D_FF], proj[..., D_FF:]
    gate, new_buf = causal_conv(gate, buf, conv_w, conv_b)
    return (jax.nn.gelu(gate, approximate=False) * val) @ w_out, new_buf


def setup_inputs(seed: int = 0) -> dict:
    key = jax.random.key(seed)
    ks = jax.random.split(key, 32)

    def nrm(k, shape, scale):
        return jax.random.normal(k, shape, jnp.float32) * scale

    dt = jnp.exp(jax.random.uniform(ks[13], (N_GDN, GDN_HEADS), jnp.float32, math.log(1e-3), math.log(1e-1)))
    a0 = jax.random.uniform(ks[22], (N_LRU, D_RNN), jnp.float32, 0.9, 0.999)
    p = a0 ** (1.0 / LRU_C)
    return {
        "x_prompt": nrm(ks[0], (BATCH, SEQ, D_MODEL), 1.0),
        "x_sample": nrm(ks[1], (DEC_BATCH, DEC_SEQ, D_MODEL), 1.0),
        "state_gdn_S": nrm(ks[2], (N_GDN, DEC_BATCH, GDN_HEADS, GDN_DK, GDN_DV), GDN_DK ** -0.5),
        "state_gdn_conv": nrm(ks[3], (N_GDN, DEC_BATCH, SHORT_CONV - 1, GDN_CONV_DIM), 1.0),
        "state_lru_h": nrm(ks[4], (N_LRU, DEC_BATCH, D_RNN), 0.5),
        "state_lru_conv": nrm(ks[5], (N_LRU, DEC_BATCH, SHORT_CONV - 1, D_RNN), 1.0),
        "state_ffn_conv": nrm(ks[6], (DEPTH, DEC_BATCH, FFN_CONV - 1, D_FF), 1.0),
        "norm_mixer": 1.0 + nrm(ks[7], (DEPTH, D_MODEL), 0.02),
        "norm_ffn": 1.0 + nrm(ks[8], (DEPTH, D_MODEL), 0.02),
        "norm_final": 1.0 + nrm(ks[9], (D_MODEL,), 0.02),
        "gdn_w_in": nrm(ks[10], (N_GDN, D_MODEL, GDN_CONV_DIM + GDN_VAL_DIM + 2 * GDN_HEADS), D_MODEL ** -0.5),
        "gdn_conv_w": nrm(ks[11], (N_GDN, SHORT_CONV, GDN_CONV_DIM), SHORT_CONV ** -0.5),
        "gdn_A_log": jnp.log(jax.random.uniform(ks[12], (N_GDN, GDN_HEADS), jnp.float32, 1.0, 16.0)),
        "gdn_dt_bias": dt + jnp.log(-jnp.expm1(-dt)),
        "gdn_norm_w": 1.0 + nrm(ks[14], (N_GDN, GDN_DV), 0.02),
        "gdn_w_out": nrm(ks[15], (N_GDN, GDN_VAL_DIM, D_MODEL), GDN_VAL_DIM ** -0.5),
        "lru_w_in": nrm(ks[16], (N_LRU, D_MODEL, 2 * D_RNN), D_MODEL ** -0.5),
        "lru_b_in": nrm(ks[17], (N_LRU, 2 * D_RNN), 0.02),
        "lru_conv_w": nrm(ks[18], (N_LRU, SHORT_CONV, D_RNN), SHORT_CONV ** -0.5),
        "lru_conv_b": nrm(ks[19], (N_LRU, D_RNN), 0.02),
        "lru_w_gate": nrm(ks[20], (N_LRU, LRU_BLOCKS, LRU_BW, 2 * LRU_BW), LRU_BW ** -0.5),
        "lru_b_gate": nrm(ks[21], (N_LRU, 2 * D_RNN), 0.02),
        "lru_lambda": jnp.log(p) - jnp.log1p(-p),
        "lru_w_out": nrm(ks[23], (N_LRU, D_RNN, D_MODEL), D_RNN ** -0.5),
        "ffn_w_in": nrm(ks[24], (DEPTH, D_MODEL, 2 * D_FF), D_MODEL ** -0.5),
        "ffn_conv_w": nrm(ks[25], (DEPTH, FFN_CONV, D_FF), FFN_CONV ** -0.5),
        "ffn_conv_b": nrm(ks[26], (DEPTH, D_FF), 0.02),
        "ffn_w_out": nrm(ks[27], (DEPTH, D_FF, D_MODEL), D_FF ** -0.5),
    }


def reference(x_prompt, x_sample, state_gdn_S, state_gdn_conv, state_lru_h, state_lru_conv, state_ffn_conv,
              norm_mixer, norm_ffn, norm_final,
              gdn_w_in, gdn_conv_w, gdn_A_log, gdn_dt_bias, gdn_norm_w, gdn_w_out,
              lru_w_in, lru_b_in, lru_conv_w, lru_conv_b, lru_w_gate, lru_b_gate, lru_lambda, lru_w_out,
              ffn_w_in, ffn_conv_w, ffn_conv_b, ffn_w_out):
    xp, xs = x_prompt, x_sample
    Bp = xp.shape[0]
    dt_ = xp.dtype
    gS_p, gS_s, gc_p, gc_s = [], [], [], []
    lh_p, lh_s, lc_p, lc_s = [], [], [], []
    fc_p, fc_s = [], []
    for i in range(DEPTH):
        j = i // N_MIXERS
        hp = rmsnorm(xp, norm_mixer[i])
        hs = rmsnorm(xs, norm_mixer[i])
        if i % N_MIXERS == 0:
            w = (gdn_w_in[j], gdn_conv_w[j], gdn_A_log[j], gdn_dt_bias[j], gdn_norm_w[j], gdn_w_out[j])
            S0p = jnp.zeros((Bp, GDN_HEADS, GDN_DK, GDN_DV), dt_)
            c0p = jnp.zeros((Bp, SHORT_CONV - 1, GDN_CONV_DIM), dt_)
            op, Sp, cp = gdn_mixer(hp, S0p, c0p, *w, chunked=True)
            os_, Ss, cs = gdn_mixer(hs, state_gdn_S[j], state_gdn_conv[j], *w, chunked=False)
            gS_p.append(Sp); gS_s.append(Ss); gc_p.append(cp); gc_s.append(cs)
        else:
            w = (lru_w_in[j], lru_b_in[j], lru_conv_w[j], lru_conv_b[j], lru_w_gate[j], lru_b_gate[j],
                 lru_lambda[j], lru_w_out[j])
            h0p = jnp.zeros((Bp, D_RNN), dt_)
            c0p = jnp.zeros((Bp, SHORT_CONV - 1, D_RNN), dt_)
            op, hp_last, cp = rglru_mixer(hp, h0p, c0p, *w)
            os_, hs_last, cs = rglru_mixer(hs, state_lru_h[j], state_lru_conv[j], *w)
            lh_p.append(hp_last); lh_s.append(hs_last); lc_p.append(cp); lc_s.append(cs)
        xp = xp + op
        xs = xs + os_
        wf = (ffn_w_in[i], ffn_conv_w[i], ffn_conv_b[i], ffn_w_out[i])
        fp, fbp = conv_ffn(rmsnorm(xp, norm_ffn[i]), jnp.zeros((Bp, FFN_CONV - 1, D_FF), dt_), *wf)
        fs, fbs = conv_ffn(rmsnorm(xs, norm_ffn[i]), state_ffn_conv[i], *wf)
        fc_p.append(fbp); fc_s.append(fbs)
        xp = xp + fp
        xs = xs + fs
    y_prompt = rmsnorm(xp, norm_final)
    y_sample = rmsnorm(xs, norm_final)
    return (y_prompt, y_sample,
            jnp.stack(gS_p), jnp.stack(gS_s), jnp.stack(gc_p), jnp.stack(gc_s),
            jnp.stack(lh_p), jnp.stack(lh_s), jnp.stack(lc_p), jnp.stack(lc_s),
            jnp.stack(fc_p), jnp.stack(fc_s))
```

```python
import functools
import math

import jax
import jax.numpy as jnp
from jax import lax
from jax.experimental import pallas as pl
from jax.experimental.pallas import tpu as pltpu

F32 = jnp.float32
BF16 = jnp.bfloat16
HIGHEST = lax.Precision.HIGHEST

NORM_EPS = 1e-6
L2_EPS = 1e-6
LRU_C = 8.0
GDN_CHUNK = 64

LANES = 128
SUBLANES = 8
VMEM_LIMIT_BYTES = 56 * 1024 * 1024


def _params(*sem):
    return pltpu.CompilerParams(dimension_semantics=sem, vmem_limit_bytes=VMEM_LIMIT_BYTES)


def _pick(n, target, align):
    best = None
    t = align
    while t <= min(n, target):
        if n % t == 0:
            best = t
        t += align
    return best if best is not None else n


def _dot(a, b, precision=None):
    return jnp.dot(a, b, preferred_element_type=F32, precision=precision)


def _dot_nt(a, b, precision=None):
    return lax.dot_general(a, b, (((1,), (1,)), ((), ())), preferred_element_type=F32, precision=precision)


def _softplus(x):
    return jnp.maximum(x, 0.0) + jnp.log1p(jnp.exp(-jnp.abs(x)))


def _sigmoid(x):
    return 1.0 / (1.0 + jnp.exp(-x))


def _silu(x):
    return x * _sigmoid(x)


def _gelu_tanh(x):
    c = math.sqrt(2.0 / math.pi)
    return 0.5 * x * (1.0 + jnp.tanh(c * (x + 0.044715 * (x * x * x))))


def _gelu_erf(x):
    return 0.5 * x * (1.0 + lax.erf(x * (1.0 / math.sqrt(2.0))))


def _rmsnorm_body(x_ref, w_ref, o_ref):
    x = x_ref[...]
    y = x * lax.rsqrt(jnp.mean(x * x, axis=-1, keepdims=True) + NORM_EPS)
    o_ref[...] = (y * w_ref[...]).astype(o_ref.dtype)


def _rmsnorm(x, w, out_dtype):
    m, d = x.shape
    tr = _pick(m, 512, 16)
    return pl.pallas_call(
        _rmsnorm_body,
        grid=(m // tr,),
        in_specs=[pl.BlockSpec((tr, d), lambda i: (i, 0)), pl.BlockSpec((1, d), lambda i: (0, 0))],
        out_specs=pl.BlockSpec((tr, d), lambda i: (i, 0)),
        out_shape=jax.ShapeDtypeStruct((m, d), out_dtype),
        compiler_params=_params("parallel"),
        name="rmsnorm",
    )(x, w.reshape(1, d))


def _matmul_body(*refs, nk, has_bias, has_res):
    x_ref, w_ref = refs[0], refs[1]
    pos = 2
    b_ref = r_ref = None
    if has_bias:
        b_ref = refs[pos]
        pos += 1
    if has_res:
        r_ref = refs[pos]
        pos += 1
    o_ref = refs[pos]

    def finish(acc):
        if has_bias:
            acc = acc + b_ref[...]
        if has_res:
            acc = r_ref[...] + acc
        o_ref[...] = acc.astype(o_ref.dtype)

    part = _dot(x_ref[...], w_ref[...])
    if nk == 1:
        finish(part)
    else:
        acc_ref = refs[pos + 1]
        k = pl.program_id(2)

        @pl.when(k == 0)
        def _():
            acc_ref[...] = part

        @pl.when(k > 0)
        def _():
            acc_ref[...] += part

        @pl.when(k == nk - 1)
        def _():
            finish(acc_ref[...])


def _matmul(x, w, bias=None, residual=None, out_dtype=F32, tm=832, tn=1024, tk=4096):
    m, kd = x.shape
    n = w.shape[1]
    tm = _pick(m, tm, 16)
    tn = _pick(n, tn, LANES)
    tk = _pick(kd, tk, LANES)
    nk = kd // tk
    in_specs = [pl.BlockSpec((tm, tk), lambda j, i, k: (i, k)), pl.BlockSpec((tk, tn), lambda j, i, k: (k, j))]
    args = [x, w]
    if bias is not None:
        in_specs.append(pl.BlockSpec((1, tn), lambda j, i, k: (0, j)))
        args.append(bias.reshape(1, n).astype(F32))
    if residual is not None:
        in_specs.append(pl.BlockSpec((tm, tn), lambda j, i, k: (i, j)))
        args.append(residual)
    body = functools.partial(_matmul_body, nk=nk, has_bias=bias is not None, has_res=residual is not None)
    return pl.pallas_call(
        body,
        grid=(n // tn, m // tm, nk),
        in_specs=in_specs,
        out_specs=pl.BlockSpec((tm, tn), lambda j, i, k: (i, j)),
        out_shape=jax.ShapeDtypeStruct((m, n), out_dtype),
        scratch_shapes=[pltpu.VMEM((tm, tn), F32)] if nk > 1 else [],
        compiler_params=_params("parallel", "parallel", "arbitrary"),
        name="matmul",
    )(*args)


def _causal_conv_rows(x, prev8, w):
    r = x.shape[0]
    kc = w.shape[0]
    full = jnp.concatenate([prev8, x], axis=0)
    y = None
    for kk in range(kc):
        off = SUBLANES - (kc - 1) + kk
        term = full[off:off + r] * w[kk:kk + 1]
        y = term if y is None else y + term
    return y


def _conv_from_state(state_ref, x, w):
    kc = w.shape[0]
    y = state_ref[0] * w[0:1]
    for kk in range(1, kc - 1):
        y = y + state_ref[kk] * w[kk:kk + 1]
    return y + x * w[kc - 1:kc]


def _gdn_gate_body(a_ref, b_ref, alog_ref, dtb_ref, e_ref, *out_refs, chunk, cumulative):
    g = -jnp.exp(alog_ref[...]) * _softplus(a_ref[...] + dtb_ref[...])
    beta = _sigmoid(b_ref[...])
    e = e_ref[...]
    if cumulative:
        r = g.shape[0]
        ri = lax.broadcasted_iota(jnp.int32, (chunk, chunk), 0)
        ci = lax.broadcasted_iota(jnp.int32, (chunk, chunk), 1)
        tril = (ri >= ci).astype(F32)
        ones = jnp.ones((chunk, chunk), F32)
        gcs, gls = [], []
        for c in range(r // chunk):
            gch = g[c * chunk:(c + 1) * chunk]
            gcs.append(_dot(tril, gch, HIGHEST))
            gls.append(_dot(ones, gch, HIGHEST))
        gc = jnp.concatenate(gcs, axis=0)
        gl = jnp.concatenate(gls, axis=0)
        out_refs[0][...] = _dot(gc, e, HIGHEST)
        out_refs[1][...] = _dot(gl, e, HIGHEST)
        out_refs[2][...] = _dot(beta, e, HIGHEST)
    else:
        out_refs[0][...] = _dot(g, e, HIGHEST)
        out_refs[1][...] = _dot(beta, e, HIGHEST)


def _gdn_gates(ab, alog_pad, dtb_pad, expand, row0, rows, cumulative):
    hd = expand.shape[1]
    tr = _pick(rows, 256, GDN_CHUNK)
    cw = _pick(hd, 1024, LANES)
    rb0 = row0 // tr
    assert row0 % tr == 0
    n_out = 3 if cumulative else 2
    body = functools.partial(_gdn_gate_body, chunk=GDN_CHUNK, cumulative=cumulative)
    return pl.pallas_call(
        body,
        grid=(rows // tr, hd // cw),
        in_specs=[
            pl.BlockSpec((tr, LANES), lambda i, j: (rb0 + i, 0)),
            pl.BlockSpec((tr, LANES), lambda i, j: (rb0 + i, 1)),
            pl.BlockSpec((1, LANES), lambda i, j: (0, 0)),
            pl.BlockSpec((1, LANES), lambda i, j: (0, 0)),
            pl.BlockSpec((LANES, cw), lambda i, j: (0, j)),
        ],
        out_specs=[pl.BlockSpec((tr, cw), lambda i, j: (i, j))] * n_out,
        out_shape=[jax.ShapeDtypeStruct((rows, hd), F32)] * n_out,
        compiler_params=_params("parallel", "parallel"),
        name="gdn_gates",
    )(ab, ab, alog_pad, dtb_pad, expand)


def _gated_head_norm(o, z, nw):
    o = o * lax.rsqrt(jnp.mean(o * o, axis=-1, keepdims=True) + NORM_EPS) * nw
    return o * _silu(z)


def _gdn_chunk(q, k, v, gcb, glb, beb, s, causal, strict, eye):
    c = q.shape[0]
    dk = k.shape[1]
    gc_row = jnp.transpose(gcb)[:c, :]
    diff = gcb[:, :c] - gc_row
    decay = jnp.where(causal, jnp.exp(jnp.where(causal, diff, 0.0)), 0.0)
    eg = jnp.exp(gcb)
    kb = k * beb
    vb = v * beb
    a = jnp.where(strict, _dot_nt(kb, k, HIGHEST) * decay, 0.0)
    n = -a
    tm = eye + n
    p = n
    for _ in range(int(math.log2(c)) - 1):
        p = _dot(p, p, HIGHEST)
        tm = tm + _dot(tm, p, HIGHEST)
    uw = _dot(tm, jnp.concatenate([vb, kb * eg], axis=1), HIGHEST)
    u = uw[:, :v.shape[1]]
    w = uw[:, v.shape[1]:]
    qk = _dot_nt(q, k, HIGHEST) * decay
    ws = _dot(jnp.concatenate([w, q * eg], axis=0), s, HIGHEST)
    v_new = u - ws[:c]
    o = ws[c:] + _dot(qk, v_new, HIGHEST)
    k_tail = k * jnp.exp(glb - gcb)
    s_decay = jnp.exp(jnp.concatenate([glb] * (dk // c), axis=0))
    s_new = s * s_decay + _dot(jnp.transpose(k_tail), v_new, HIGHEST)
    return o, s_new


def _gdn_prompt_body(q_ref, k_ref, v_ref, z_ref, gc_ref, gl_ref, be_ref, wq_ref, wk_ref, wv_ref, nw_ref,
                     o_ref, s_out_ref, s_scr, carry_scr, *, heads, chunk, q_scale):
    t = pl.program_id(2)

    @pl.when(t == 0)
    def _():
        s_scr[...] = jnp.zeros_like(s_scr)
        carry_scr[...] = jnp.zeros_like(carry_scr)

    r = q_ref.shape[0]

    def conv_silu(x_ref, w_ref, idx):
        x = x_ref[...]
        y = _causal_conv_rows(x, carry_scr[idx], w_ref[...])
        carry_scr[idx] = x[r - SUBLANES:]
        return _silu(y)

    qc = conv_silu(q_ref, wq_ref, 0)
    kc = conv_silu(k_ref, wk_ref, 1)
    vc = conv_silu(v_ref, wv_ref, 2)

    ri = lax.broadcasted_iota(jnp.int32, (chunk, chunk), 0)
    ci = lax.broadcasted_iota(jnp.int32, (chunk, chunk), 1)
    causal = ri >= ci
    strict = ri > ci
    eye = (ri == ci).astype(F32)
    nw = nw_ref[...]

    for h in range(heads):
        ls = slice(h * LANES, (h + 1) * LANES)
        qh = qc[:, ls]
        kh = kc[:, ls]
        qh = qh * lax.rsqrt(jnp.sum(qh * qh, axis=-1, keepdims=True) + L2_EPS) * q_scale
        kh = kh * lax.rsqrt(jnp.sum(kh * kh, axis=-1, keepdims=True) + L2_EPS)
        vh = vc[:, ls]
        s = s_scr[h]
        for c in range(r // chunk):
            rs = slice(c * chunk, (c + 1) * chunk)
            o, s = _gdn_chunk(qh[rs], kh[rs], vh[rs], gc_ref[rs, ls], gl_ref[rs, ls], be_ref[rs, ls], s,
                              causal, strict, eye)
            o_ref[rs, ls] = _gated_head_norm(o, z_ref[rs, ls], nw).astype(o_ref.dtype)
        s_scr[h] = s
        s_out_ref[0, h] = s


def _gdn_prompt(proj, gcb, glb, beb, conv_w, norm_w, batch, seq, n_heads, rows_per_step, heads_per_step):
    hd = n_heads * LANES
    g = heads_per_step
    gw = g * LANES
    r = rows_per_step
    nt = seq // r
    ncb = hd // gw
    row = lambda b, hg, t: b * nt + t
    sect = lambda s: pl.BlockSpec((r, gw), lambda b, hg, t: (row(b, hg, t), s * ncb + hg))
    gate = pl.BlockSpec((r, gw), lambda b, hg, t: (row(b, hg, t), hg))
    wsp = lambda s: pl.BlockSpec((conv_w.shape[0], gw), lambda b, hg, t: (0, s * ncb + hg))
    body = functools.partial(_gdn_prompt_body, heads=g, chunk=GDN_CHUNK, q_scale=float(LANES) ** -0.5)
    return pl.pallas_call(
        body,
        grid=(batch, n_heads // g, nt),
        in_specs=[sect(0), sect(1), sect(2), sect(3), gate, gate, gate, wsp(0), wsp(1), wsp(2),
                  pl.BlockSpec((1, LANES), lambda b, hg, t: (0, 0))],
        out_specs=[
            pl.BlockSpec((r, gw), lambda b, hg, t: (row(b, hg, t), hg)),
            pl.BlockSpec((1, g, LANES, LANES), lambda b, hg, t: (b, hg, 0, 0)),
        ],
        out_shape=[
            jax.ShapeDtypeStruct((batch * seq, hd), BF16),
            jax.ShapeDtypeStruct((batch, n_heads, LANES, LANES), F32),
        ],
        scratch_shapes=[pltpu.VMEM((g, LANES, LANES), F32), pltpu.VMEM((3, SUBLANES, gw), F32)],
        compiler_params=_params("parallel", "parallel", "arbitrary"),
        name="gdn_prompt",
    )(proj, proj, proj, proj, gcb, glb, beb, conv_w, conv_w, conv_w, norm_w.reshape(1, LANES))


def _gdn_sample_body(q_ref, k_ref, v_ref, z_ref, sq_ref, sk_ref, sv_ref, g_ref, be_ref, wq_ref, wk_ref, wv_ref,
                     nw_ref, s_ref, o_ref, s_out_ref, q_scr, k_scr, d_scr, o_scr, *, q_scale):
    nb = q_ref.shape[0]
    q = _silu(_conv_from_state(sq_ref, q_ref[...], wq_ref[...]))
    k = _silu(_conv_from_state(sk_ref, k_ref[...], wk_ref[...]))
    v = _silu(_conv_from_state(sv_ref, v_ref[...], wv_ref[...]))
    q_scr[...] = q * lax.rsqrt(jnp.sum(q * q, axis=-1, keepdims=True) + L2_EPS) * q_scale
    k_scr[...] = k * lax.rsqrt(jnp.sum(k * k, axis=-1, keepdims=True) + L2_EPS)
    d_scr[0] = jnp.exp(g_ref[...])
    d_scr[1] = be_ref[...]
    d_scr[2] = v

    def step(b, carry):
        row = pl.ds(b, 1)
        k_col = jnp.transpose(jnp.broadcast_to(k_scr[row, :], (LANES, LANES)))
        q_col = jnp.transpose(jnp.broadcast_to(q_scr[row, :], (LANES, LANES)))
        s = s_ref[b] * d_scr[0, row, :]
        ks = jnp.sum(k_col * s, axis=0, keepdims=True)
        delta = d_scr[1, row, :] * (d_scr[2, row, :] - ks)
        s = s + k_col * delta
        s_out_ref[b] = s
        o_scr[row, :] = jnp.sum(q_col * s, axis=0, keepdims=True)
        return carry

    lax.fori_loop(0, nb, step, 0)
    o_ref[...] = _gated_head_norm(o_scr[...], z_ref[...], nw_ref[...]).astype(o_ref.dtype)


def _gdn_sample(proj, conv_state, gb, beb, conv_w, norm_w, s0, row0, n_heads, samples_per_step):
    nsamp = s0.shape[0]
    bs = samples_per_step
    assert row0 % bs == 0 and nsamp % bs == 0
    rb0 = row0 // bs
    nh = n_heads
    kc = conv_w.shape[0]
    sect = lambda s: pl.BlockSpec((bs, LANES), lambda h, i: (rb0 + i, s * nh + h))
    stat = lambda s: pl.BlockSpec((kc - 1, bs, LANES), lambda h, i: (0, i, s * nh + h))
    gate = pl.BlockSpec((bs, LANES), lambda h, i: (i, h))
    wsp = lambda s: pl.BlockSpec((kc, LANES), lambda h, i: (0, s * nh + h))
    sspec = pl.BlockSpec((bs, None, LANES, LANES), lambda h, i: (i, h, 0, 0))
    body = functools.partial(_gdn_sample_body, q_scale=float(LANES) ** -0.5)
    return pl.pallas_call(
        body,
        grid=(nh, nsamp // bs),
        in_specs=[sect(0), sect(1), sect(2), sect(3), stat(0), stat(1), stat(2), gate, gate,
                  wsp(0), wsp(1), wsp(2), pl.BlockSpec((1, LANES), lambda h, i: (0, 0)), sspec],
        out_specs=[pl.BlockSpec((bs, LANES), lambda h, i: (i, h)), sspec],
        out_shape=[
            jax.ShapeDtypeStruct((nsamp, nh * LANES), BF16),
            jax.ShapeDtypeStruct(s0.shape, F32),
        ],
        scratch_shapes=[pltpu.VMEM((bs, LANES), F32), pltpu.VMEM((bs, LANES), F32),
                        pltpu.VMEM((3, bs, LANES), F32), pltpu.VMEM((bs, LANES), F32)],
        compiler_params=_params("parallel", "parallel"),
        name="gdn_sample",
    )(proj, proj, proj, proj, conv_state, conv_state, conv_state, gb, beb, conv_w, conv_w, conv_w,
      norm_w.reshape(1, LANES), s0)


def _lru_coeffs(xc, wg_ref, bgr_ref, bgi_ref, lam_ref):
    bw = xc.shape[1]
    gates = _dot(xc.astype(BF16), wg_ref[0])
    r = _sigmoid(gates[:, :bw] + bgr_ref[...])
    i = _sigmoid(gates[:, bw:] + bgi_ref[...])
    log_a = -LRU_C * r * _softplus(-lam_ref[...])
    a = jnp.exp(log_a)
    mult = jnp.sqrt(-jnp.tanh(log_a) * (a * a + 1.0))
    return a, mult * (i * xc)


def _scan8(a, b):
    row = lax.broadcasted_iota(jnp.int32, a.shape, 0)
    for s in (1, 2, 4):
        a_sh = pltpu.roll(a, s, 0)
        b_sh = pltpu.roll(b, s, 0)
        m = row >= s
        b = jnp.where(m, a * b_sh + b, b)
        a = jnp.where(m, a * a_sh, a)
    return a, b


def _lru_prompt_body(x_ref, y_ref, cw_ref, cb_ref, wg_ref, bgr_ref, bgi_ref, lam_ref,
                     o_ref, h_out_ref, carry_scr, h_scr, a_scr, b_scr):
    t = pl.program_id(2)

    @pl.when(t == 0)
    def _():
        carry_scr[...] = jnp.zeros_like(carry_scr)
        h_scr[...] = jnp.zeros_like(h_scr)

    x = x_ref[...]
    r = x.shape[0]
    xc = _causal_conv_rows(x, carry_scr[...], cw_ref[...]) + cb_ref[...]
    carry_scr[...] = x[r - SUBLANES:]
    a, b = _lru_coeffs(xc, wg_ref, bgr_ref, bgi_ref, lam_ref)
    a_scr[...] = a
    b_scr[...] = b

    def group(i, h_prev):
        rows = pl.ds(pl.multiple_of(i * SUBLANES, SUBLANES), SUBLANES)
        a8, b8 = _scan8(a_scr[rows, :], b_scr[rows, :])
        h8 = a8 * h_prev + b8
        b_scr[rows, :] = h8
        return jnp.broadcast_to(h8[SUBLANES - 1:SUBLANES, :], h8.shape)

    h_last = lax.fori_loop(0, r // SUBLANES, group, h_scr[...])
    h_scr[...] = h_last
    h_out_ref[0] = h_last[0:1, :]
    o_ref[...] = (b_scr[...] * _gelu_tanh(y_ref[...])).astype(o_ref.dtype)


def _lru_prompt(proj, conv_w, conv_b, w_gate, b_gate, lam, batch, seq, rows_per_step):
    nb, bw = w_gate.shape[0], w_gate.shape[1]
    drnn = nb * bw
    r = rows_per_step
    nt = seq // r
    row = lambda b, n, t: b * nt + t
    vec = lambda off: pl.BlockSpec((1, bw), lambda b, n, t: (0, off + n))
    return pl.pallas_call(
        _lru_prompt_body,
        grid=(batch, nb, nt),
        in_specs=[
            pl.BlockSpec((r, bw), lambda b, n, t: (row(b, n, t), n)),
            pl.BlockSpec((r, bw), lambda b, n, t: (row(b, n, t), nb + n)),
            pl.BlockSpec((conv_w.shape[0], bw), lambda b, n, t: (0, n)),
            vec(0),
            pl.BlockSpec((1, bw, 2 * bw), lambda b, n, t: (n, 0, 0)),
            vec(0), vec(nb), vec(0),
        ],
        out_specs=[
            pl.BlockSpec((r, bw), lambda b, n, t: (row(b, n, t), n)),
            pl.BlockSpec((1, 1, bw), lambda b, n, t: (b, 0, n)),
        ],
        out_shape=[
            jax.ShapeDtypeStruct((batch * seq, drnn), BF16),
            jax.ShapeDtypeStruct((batch, 1, drnn), F32),
        ],
        scratch_shapes=[pltpu.VMEM((SUBLANES, bw), F32), pltpu.VMEM((SUBLANES, bw), F32),
                        pltpu.VMEM((r, bw), F32), pltpu.VMEM((r, bw), F32)],
        compiler_params=_params("parallel", "parallel", "arbitrary"),
        name="lru_prompt",
    )(proj, proj, conv_w, conv_b.reshape(1, drnn), w_gate, b_gate.reshape(1, 2 * drnn),
      b_gate.reshape(1, 2 * drnn), lam.reshape(1, drnn))


def _lru_sample_body(x_ref, y_ref, st_ref, h0_ref, cw_ref, cb_ref, wg_ref, bgr_ref, bgi_ref, lam_ref,
                     o_ref, h_out_ref):
    xc = _conv_from_state(st_ref, x_ref[...], cw_ref[...]) + cb_ref[...]
    a, b = _lru_coeffs(xc, wg_ref, bgr_ref, bgi_ref, lam_ref)
    h = a * h0_ref[...] + b
    h_out_ref[...] = h
    o_ref[...] = (h * _gelu_tanh(y_ref[...])).astype(o_ref.dtype)


def _lru_sample(proj, conv_state, h0, conv_w, conv_b, w_gate, b_gate, lam, row0):
    nsamp = h0.shape[0]
    nb, bw = w_gate.shape[0], w_gate.shape[1]
    drnn = nb * bw
    kc = conv_w.shape[0]
    assert row0 % nsamp == 0
    rb0 = row0 // nsamp
    vec = lambda off: pl.BlockSpec((1, bw), lambda n: (0, off + n))
    return pl.pallas_call(
        _lru_sample_body,
        grid=(nb,),
        in_specs=[
            pl.BlockSpec((nsamp, bw), lambda n: (rb0, n)),
            pl.BlockSpec((nsamp, bw), lambda n: (rb0, nb + n)),
            pl.BlockSpec((kc - 1, nsamp, bw), lambda n: (0, 0, n)),
            pl.BlockSpec((nsamp, bw), lambda n: (0, n)),
            pl.BlockSpec((kc, bw), lambda n: (0, n)),
            vec(0),
            pl.BlockSpec((1, bw, 2 * bw), lambda n: (n, 0, 0)),
            vec(0), vec(nb), vec(0),
        ],
        out_specs=[pl.BlockSpec((nsamp, bw), lambda n: (0, n))] * 2,
        out_shape=[jax.ShapeDtypeStruct((nsamp, drnn), BF16), jax.ShapeDtypeStruct((nsamp, drnn), F32)],
        compiler_params=_params("parallel"),
        name="lru_sample",
    )(proj, proj, conv_state, h0, conv_w, conv_b.reshape(1, drnn), w_gate, b_gate.reshape(1, 2 * drnn),
      b_gate.reshape(1, 2 * drnn), lam.reshape(1, drnn))


def _ffn_prompt_body(g_ref, v_ref, cw_ref, cb_ref, o_ref, carry_scr):
    t = pl.program_id(2)

    @pl.when(t == 0)
    def _():
        carry_scr[...] = jnp.zeros_like(carry_scr)

    g = g_ref[...]
    r = g.shape[0]
    gate = _causal_conv_rows(g, carry_scr[...], cw_ref[...]) + cb_ref[...]
    carry_scr[...] = g[r - SUBLANES:]
    o_ref[...] = (_gelu_erf(gate) * v_ref[...]).astype(o_ref.dtype)


def _ffn_prompt(proj, conv_w, conv_b, batch, seq, rows_per_step, cols_per_step):
    dff = conv_w.shape[1]
    r, tc = rows_per_step, cols_per_step
    nt, nc = seq // r, dff // tc
    row = lambda b, c, t: b * nt + t
    return pl.pallas_call(
        _ffn_prompt_body,
        grid=(batch, nc, nt),
        in_specs=[
            pl.BlockSpec((r, tc), lambda b, c, t: (row(b, c, t), c)),
            pl.BlockSpec((r, tc), lambda b, c, t: (row(b, c, t), nc + c)),
            pl.BlockSpec((conv_w.shape[0], tc), lambda b, c, t: (0, c)),
            pl.BlockSpec((1, tc), lambda b, c, t: (0, c)),
        ],
        out_specs=pl.BlockSpec((r, tc), lambda b, c, t: (row(b, c, t), c)),
        out_shape=jax.ShapeDtypeStruct((batch * seq, dff), BF16),
        scratch_shapes=[pltpu.VMEM((SUBLANES, tc), F32)],
        compiler_params=_params("parallel", "parallel", "arbitrary"),
        name="ffn_prompt",
    )(proj, proj, conv_w, conv_b.reshape(1, dff))


def _ffn_sample_body(g_ref, v_ref, st_ref, cw_ref, cb_ref, o_ref):
    gate = _conv_from_state(st_ref, g_ref[...], cw_ref[...]) + cb_ref[...]
    o_ref[...] = (_gelu_erf(gate) * v_ref[...]).astype(o_ref.dtype)


def _ffn_sample(proj, conv_state, conv_w, conv_b, row0, cols_per_step):
    nsamp = conv_state.shape[1]
    dff = conv_w.shape[1]
    kc = conv_w.shape[0]
    tc = cols_per_step
    nc = dff // tc
    assert row0 % nsamp == 0
    rb0 = row0 // nsamp
    return pl.pallas_call(
        _ffn_sample_body,
        grid=(nc,),
        in_specs=[
            pl.BlockSpec((nsamp, tc), lambda c: (rb0, c)),
            pl.BlockSpec((nsamp, tc), lambda c: (rb0, nc + c)),
            pl.BlockSpec((kc - 1, nsamp, tc), lambda c: (0, 0, c)),
            pl.BlockSpec((kc, tc), lambda c: (0, c)),
            pl.BlockSpec((1, tc), lambda c: (0, c)),
        ],
        out_specs=pl.BlockSpec((nsamp, tc), lambda c: (0, c)),
        out_shape=jax.ShapeDtypeStruct((nsamp, dff), BF16),
        compiler_params=_params("parallel"),
        name="ffn_sample",
    )(proj, proj, conv_state, conv_w, conv_b.reshape(1, dff))


def _new_conv_state(old_state, pre_rows):
    return jnp.concatenate([old_state[:, 1:], pre_rows[:, None, :]], axis=1)


def _gdn_layer(x, h, mp, batch, seq, s0, conv_state, w_in, conv_w, a_log, dt_bias, norm_w, w_out):
    n_heads = a_log.shape[0]
    d_model = x.shape[1]
    vd = w_out.shape[0]
    cd = conv_w.shape[1]
    nsamp = s0.shape[0]
    assert vd == n_heads * LANES and cd == 3 * vd, "kernels assume DK == DV == 128"
    kc = conv_w.shape[0]

    w_main = w_in[:, :cd + vd].astype(BF16)
    w_ab = jnp.zeros((d_model, 2 * LANES), F32)
    w_ab = w_ab.at[:, :n_heads].set(w_in[:, cd + vd:cd + vd + n_heads])
    w_ab = w_ab.at[:, LANES:LANES + n_heads].set(w_in[:, cd + vd + n_heads:])
    pad = lambda p: jnp.zeros((1, LANES), F32).at[0, :n_heads].set(p.astype(F32))
    expand = (jnp.arange(LANES)[:, None] == (jnp.arange(vd) // LANES)[None, :]).astype(F32)

    proj = _matmul(h, w_main)
    ab = _matmul(h, w_ab.astype(BF16))

    gcb, glb, beb = _gdn_gates(ab, pad(a_log), pad(dt_bias), expand, 0, mp, True)
    o_p, s_p = _gdn_prompt(proj, gcb, glb, beb, conv_w, norm_w, batch, seq, n_heads,
                           rows_per_step=min(seq, 256), heads_per_step=min(n_heads, 2))

    gb_s, beb_s = _gdn_gates(ab, pad(a_log), pad(dt_bias), expand, mp, nsamp, False)
    state_t = jnp.transpose(conv_state, (1, 0, 2))
    o_s, s_s = _gdn_sample(proj, state_t, gb_s, beb_s, conv_w, norm_w, s0, mp, n_heads,
                           samples_per_step=min(nsamp, 64))

    o = jnp.concatenate([o_p, o_s], axis=0)
    x = _matmul(o, w_out.astype(BF16), residual=x)

    conv_p = proj[:mp, :cd].reshape(batch, seq, cd)[:, seq - (kc - 1):]
    conv_s = _new_conv_state(conv_state, proj[mp:, :cd])
    return x, s_p, s_s, conv_p, conv_s


def _lru_layer(x, h, mp, batch, seq, h0, conv_state, w_in, b_in, conv_w, conv_b, w_gate, b_gate, lam, w_out):
    drnn = conv_w.shape[1]
    kc = conv_w.shape[0]
    proj = _matmul(h, w_in.astype(BF16), bias=b_in)
    wg = w_gate.astype(BF16)
    o_p, h_p = _lru_prompt(proj, conv_w, conv_b, wg, b_gate, lam, batch, seq, rows_per_step=min(seq, 512))
    state_t = jnp.transpose(conv_state, (1, 0, 2))
    o_s, h_s = _lru_sample(proj, state_t, h0, conv_w, conv_b, wg, b_gate, lam, mp)
    o = jnp.concatenate([o_p, o_s], axis=0)
    x = _matmul(o, w_out.astype(BF16), residual=x)
    conv_p = proj[:mp, :drnn].reshape(batch, seq, drnn)[:, seq - (kc - 1):]
    conv_s = _new_conv_state(conv_state, proj[mp:, :drnn])
    return x, h_p.reshape(batch, drnn), h_s, conv_p, conv_s


def _ffn_layer(x, h, mp, batch, seq, conv_state, w_in, conv_w, conv_b, w_out):
    dff = conv_w.shape[1]
    kc = conv_w.shape[0]
    proj = _matmul(h, w_in.astype(BF16), tn=512)
    tc = _pick(dff, 5504, LANES)
    a_p = _ffn_prompt(proj, conv_w, conv_b, batch, seq, rows_per_step=min(seq, 256), cols_per_step=tc)
    state_t = jnp.transpose(conv_state, (1, 0, 2))
    a_s = _ffn_sample(proj, state_t, conv_w, conv_b, mp, tc)
    act = jnp.concatenate([a_p, a_s], axis=0)
    x = _matmul(act, w_out.astype(BF16), residual=x, tn=512, tk=_pick(dff, 5504, LANES))
    conv_p = proj[:mp, :dff].reshape(batch, seq, dff)[:, seq - (kc - 1):]
    conv_s = _new_conv_state(conv_state, proj[mp:, :dff])
    return x, conv_p, conv_s


def kernel(x_prompt, x_sample, state_gdn_S, state_gdn_conv, state_lru_h, state_lru_conv, state_ffn_conv, norm_mixer, norm_ffn, norm_final, gdn_w_in, gdn_conv_w, gdn_A_log, gdn_dt_bias, gdn_norm_w, gdn_w_out, lru_w_in, lru_b_in, lru_conv_w, lru_conv_b, lru_w_gate, lru_b_gate, lru_lambda, lru_w_out, ffn_w_in, ffn_conv_w, ffn_conv_b, ffn_w_out):
    batch, seq, d_model = x_prompt.shape
    nsamp = x_sample.shape[0]
    depth = norm_mixer.shape[0]
    mp = batch * seq
    x = jnp.concatenate([x_prompt.reshape(mp, d_model), x_sample.reshape(nsamp, d_model)], axis=0)

    gs_p, gs_s, gc_p, gc_s = [], [], [], []
    lh_p, lh_s, lc_p, lc_s = [], [], [], []
    fc_p, fc_s = [], []
    for i in range(depth):
        j = i // 2
        h = _rmsnorm(x, norm_mixer[i], BF16)
        if i % 2 == 0:
            x, sp, ss, cp, cs = _gdn_layer(
                x, h, mp, batch, seq, state_gdn_S[j], state_gdn_conv[j], gdn_w_in[j], gdn_conv_w[j],
                gdn_A_log[j], gdn_dt_bias[j], gdn_norm_w[j], gdn_w_out[j])
            gs_p.append(sp); gs_s.append(ss); gc_p.append(cp); gc_s.append(cs)
        else:
            x, hp, hs, cp, cs = _lru_layer(
                x, h, mp, batch, seq, state_lru_h[j], state_lru_conv[j], lru_w_in[j], lru_b_in[j],
                lru_conv_w[j], lru_conv_b[j], lru_w_gate[j], lru_b_gate[j], lru_lambda[j], lru_w_out[j])
            lh_p.append(hp); lh_s.append(hs); lc_p.append(cp); lc_s.append(cs)
        h = _rmsnorm(x, norm_ffn[i], BF16)
        x, cp, cs = _ffn_layer(x, h, mp, batch, seq, state_ffn_conv[i], ffn_w_in[i], ffn_conv_w[i],
                               ffn_conv_b[i], ffn_w_out[i])
        fc_p.append(cp); fc_s.append(cs)

    y = _rmsnorm(x, norm_final, F32)
    y_prompt = y[:mp].reshape(batch, seq, d_model)
    y_sample = y[mp:].reshape(nsamp, 1, d_model)
    return (y_prompt, y_sample,
            jnp.stack(gs_p), jnp.stack(gs_s), jnp.stack(gc_p), jnp.stack(gc_s),
            jnp.stack(lh_p), jnp.stack(lh_s), jnp.stack(lc_p), jnp.stack(lc_s),
            jnp.stack(fc_p), jnp.stack(fc_s))
```

```python
import functools
import math

import jax
import jax.numpy as jnp
from jax import lax
from jax.experimental import pallas as pl
from jax.experimental.pallas import tpu as pltpu

F32 = jnp.float32
BF16 = jnp.bfloat16
HIGHEST = lax.Precision.HIGHEST

NORM_EPS = 1e-6
L2_EPS = 1e-6
LRU_C = 8.0
GDN_CHUNK = 64

LANES = 128
SUBLANES = 8
VMEM_LIMIT_BYTES = 56 * 1024 * 1024


def _params(*sem):
    return pltpu.CompilerParams(dimension_semantics=sem, vmem_limit_bytes=VMEM_LIMIT_BYTES)


def _pick(n, target, align):
    best = None
    t = align
    while t <= min(n, target):
        if n % t == 0:
            best = t
        t += align
    return best if best is not None else n


def _dot(a, b, precision=None):
    return jnp.dot(a, b, preferred_element_type=F32, precision=precision)


def _dot_nt(a, b, precision=None):
    return lax.dot_general(a, b, (((1,), (1,)), ((), ())), preferred_element_type=F32, precision=precision)


def _softplus(x):
    return jnp.maximum(x, 0.0) + jnp.log1p(jnp.exp(-jnp.abs(x)))


def _sigmoid(x):
    return 1.0 / (1.0 + jnp.exp(-x))


def _silu(x):
    return x * _sigmoid(x)


def _gelu_tanh(x):
    c = math.sqrt(2.0 / math.pi)
    return 0.5 * x * (1.0 + jnp.tanh(c * (x + 0.044715 * (x * x * x))))


def _gelu_erf(x):
    return 0.5 * x * (1.0 + lax.erf(x * (1.0 / math.sqrt(2.0))))


def _rmsnorm_body(x_ref, w_ref, o_ref):
    x = x_ref[...]
    y = x * lax.rsqrt(jnp.mean(x * x, axis=-1, keepdims=True) + NORM_EPS)
    o_ref[...] = (y * w_ref[...]).astype(o_ref.dtype)


def _rmsnorm(x, w, out_dtype, row0=0, rows=None):
    d = x.shape[1]
    rows = x.shape[0] if rows is None else rows
    tr = _pick(math.gcd(rows, row0) if row0 else rows, 512, 16)
    rb0 = row0 // tr
    return pl.pallas_call(
        _rmsnorm_body,
        grid=(rows // tr,),
        in_specs=[pl.BlockSpec((tr, d), lambda i: (rb0 + i, 0)), pl.BlockSpec((1, d), lambda i: (0, 0))],
        out_specs=pl.BlockSpec((tr, d), lambda i: (i, 0)),
        out_shape=jax.ShapeDtypeStruct((rows, d), out_dtype),
        compiler_params=_params("parallel"),
        name="rmsnorm",
    )(x, w.reshape(1, d))


def _matmul_body(*refs, nk, has_bias, has_res):
    x_ref, w_ref = refs[0], refs[1]
    pos = 2
    b_ref = r_ref = None
    if has_bias:
        b_ref = refs[pos]
        pos += 1
    if has_res:
        r_ref = refs[pos]
        pos += 1
    o_ref = refs[pos]

    def finish(acc):
        if has_bias:
            acc = acc + b_ref[...]
        if has_res:
            acc = r_ref[...] + acc
        o_ref[...] = acc.astype(o_ref.dtype)

    part = _dot(x_ref[...], w_ref[...])
    if nk == 1:
        finish(part)
    else:
        acc_ref = refs[pos + 1]
        k = pl.program_id(2)

        @pl.when(k == 0)
        def _():
            acc_ref[...] = part

        @pl.when(k > 0)
        def _():
            acc_ref[...] += part

        @pl.when(k == nk - 1)
        def _():
            finish(acc_ref[...])


def _matmul(x, w, layer, bias=None, residual=None, n_cols=None, out_dtype=F32, tm=832, tn=1024, tk=4096):
    m, kd = x.shape
    n = w.shape[2] if n_cols is None else n_cols
    tm = _pick(m, tm, 16)
    tn = _pick(n, tn, LANES)
    tk = _pick(kd, tk, LANES)
    nk = kd // tk
    in_specs = [pl.BlockSpec((tm, tk), lambda j, i, k: (i, k)),
                pl.BlockSpec((None, tk, tn), lambda j, i, k: (layer, k, j))]
    args = [x, w]
    if bias is not None:
        in_specs.append(pl.BlockSpec((1, tn), lambda j, i, k: (0, j)))
        args.append(bias.reshape(1, n).astype(F32))
    if residual is not None:
        in_specs.append(pl.BlockSpec((tm, tn), lambda j, i, k: (i, j)))
        args.append(residual)
    body = functools.partial(_matmul_body, nk=nk, has_bias=bias is not None, has_res=residual is not None)
    return pl.pallas_call(
        body,
        grid=(n // tn, m // tm, nk),
        in_specs=in_specs,
        out_specs=pl.BlockSpec((tm, tn), lambda j, i, k: (i, j)),
        out_shape=jax.ShapeDtypeStruct((m, n), out_dtype),
        scratch_shapes=[pltpu.VMEM((tm, tn), F32)] if nk > 1 else [],
        compiler_params=_params("parallel", "parallel", "arbitrary"),
        name="matmul",
    )(*args)


def _causal_conv_rows(x, prev8, w):
    r = x.shape[0]
    kc = w.shape[0]
    full = jnp.concatenate([prev8, x], axis=0)
    y = None
    for kk in range(kc):
        off = SUBLANES - (kc - 1) + kk
        term = full[off:off + r] * w[kk:kk + 1]
        y = term if y is None else y + term
    return y


def _conv_from_state(state_ref, x, w):
    kc = w.shape[0]
    y = state_ref[0] * w[0:1]
    for kk in range(1, kc - 1):
        y = y + state_ref[kk] * w[kk:kk + 1]
    return y + x * w[kc - 1:kc]


def _expand_lanes(x, e):
    hi = x.astype(BF16)
    rest = x - hi.astype(F32)
    mid = rest.astype(BF16)
    lo = (rest - mid.astype(F32)).astype(BF16)
    return _dot(hi, e) + _dot(mid, e) + _dot(lo, e)


def _gdn_gate_body(a_ref, b_ref, alog_ref, dtb_ref, e_ref, *out_refs, chunk, cumulative):
    g = -jnp.exp(alog_ref[...]) * _softplus(a_ref[...] + dtb_ref[...])
    beta = _sigmoid(b_ref[...])
    e = e_ref[...]
    if cumulative:
        r = g.shape[0]
        ri = lax.broadcasted_iota(jnp.int32, (chunk, chunk), 0)
        ci = lax.broadcasted_iota(jnp.int32, (chunk, chunk), 1)
        tril = (ri >= ci).astype(F32)
        gc = jnp.concatenate(
            [_dot(tril, g[c * chunk:(c + 1) * chunk], HIGHEST) for c in range(r // chunk)], axis=0)
        out_refs[0][...] = _expand_lanes(gc, e)
        out_refs[1][...] = _expand_lanes(beta, e)
        out_refs[2][...] = jnp.transpose(gc)
    else:
        out_refs[0][...] = _expand_lanes(g, e)
        out_refs[1][...] = _expand_lanes(beta, e)


def _gdn_gates(ab, alog_pad, dtb_pad, expand, row0, rows, cumulative):
    hd = expand.shape[1]
    tr = _pick(rows, 256, LANES)
    cw = _pick(hd, 1024, LANES)
    rb0 = row0 // tr
    assert row0 % tr == 0 and tr % GDN_CHUNK == 0
    wide = pl.BlockSpec((tr, cw), lambda i, j: (i, j))
    out_specs = [wide, wide]
    out_shape = [jax.ShapeDtypeStruct((rows, hd), F32)] * 2
    if cumulative:
        out_specs.append(pl.BlockSpec((LANES, tr), lambda i, j: (0, i)))
        out_shape.append(jax.ShapeDtypeStruct((LANES, rows), F32))
    body = functools.partial(_gdn_gate_body, chunk=GDN_CHUNK, cumulative=cumulative)
    return pl.pallas_call(
        body,
        grid=(rows // tr, hd // cw),
        in_specs=[
            pl.BlockSpec((tr, LANES), lambda i, j: (rb0 + i, 0)),
            pl.BlockSpec((tr, LANES), lambda i, j: (rb0 + i, 1)),
            pl.BlockSpec((1, LANES), lambda i, j: (0, 0)),
            pl.BlockSpec((1, LANES), lambda i, j: (0, 0)),
            pl.BlockSpec((LANES, cw), lambda i, j: (0, j)),
        ],
        out_specs=out_specs,
        out_shape=out_shape,
        compiler_params=_params("parallel", "arbitrary"),
        name="gdn_gates",
    )(ab, ab, alog_pad, dtb_pad, expand)


def _gated_head_norm(o, z, nw):
    o = o * lax.rsqrt(jnp.mean(o * o, axis=-1, keepdims=True) + NORM_EPS) * nw
    return o * _silu(z)


def _gdn_block_masks(r, chunk, dk):
    ri = lax.broadcasted_iota(jnp.int32, (r, r), 0)
    ci = lax.broadcasted_iota(jnp.int32, (r, r), 1)
    first = (ri // chunk) * chunk
    causal = (ci >= first) & (ci <= ri)
    strict = (ci >= first) & (ci < ri)
    eye = (ri == ci).astype(F32)
    bi = lax.broadcasted_iota(jnp.int32, (r // chunk * dk, r), 0) // dk
    bj = lax.broadcasted_iota(jnp.int32, (r // chunk * dk, r), 1) // chunk
    return causal, strict, eye, bi == bj


def _gdn_block_terms(q, k, v, gcb, gc_row, beb, chunk, masks):
    r = k.shape[0]
    dv = v.shape[1]
    nch = r // chunk
    causal, strict, eye, kt_mask = masks
    diff = jnp.concatenate([gcb] * (r // LANES), axis=1) - gc_row
    decay = jnp.where(causal, jnp.exp(jnp.where(causal, diff, 0.0)), 0.0)
    eg = jnp.exp(gcb)
    kb = k * beb
    kq = _dot_nt(jnp.concatenate([kb, q], axis=0).astype(BF16), k.astype(BF16))
    a = jnp.where(strict, kq[:r] * decay, 0.0)
    qk = kq[r:] * decay
    n = -a
    tm = eye + n
    p = n
    for _ in range(int(math.log2(chunk)) - 1):
        pb = p.astype(BF16)
        p = _dot(pb, pb)
        tm = tm + _dot(tm.astype(BF16), p.astype(BF16))
    uwb = _dot(tm.astype(BF16), jnp.concatenate([v * beb, kb * eg], axis=1).astype(BF16)).astype(BF16)
    qk_uw = _dot(qk.astype(BF16), uwb)
    gl = jnp.concatenate(
        [jnp.broadcast_to(gcb[(c + 1) * chunk - 1:(c + 1) * chunk, :], (chunk, LANES)) for c in range(nch)], axis=0)
    k_tail_t = jnp.transpose(k * jnp.exp(gl - gcb))
    kt_blocks = jnp.where(kt_mask, jnp.concatenate([k_tail_t] * nch, axis=0), 0.0)
    kt_uw = _dot(kt_blocks.astype(BF16), uwb)
    return q * eg - qk_uw[:, dv:], qk_uw[:, :dv], kt_uw[:, dv:], kt_uw[:, :dv], gl


def _gdn_prompt_body(q_ref, k_ref, v_ref, z_ref, gc_ref, gr_ref, be_ref, wq_ref, wk_ref, wv_ref, nw_ref,
                     o_ref, s_out_ref, s_scr, carry_scr, *, heads, chunk, q_scale):
    t = pl.program_id(2)

    @pl.when(t == 0)
    def _():
        s_scr[...] = jnp.zeros_like(s_scr)
        carry_scr[...] = jnp.zeros_like(carry_scr)

    r = q_ref.shape[0]

    def conv_silu(x_ref, w_ref, idx):
        x = x_ref[...]
        y = _causal_conv_rows(x, carry_scr[idx], w_ref[...])
        carry_scr[idx] = x[r - SUBLANES:]
        return _silu(y)

    qc = conv_silu(q_ref, wq_ref, 0)
    kc = conv_silu(k_ref, wk_ref, 1)
    vc = conv_silu(v_ref, wv_ref, 2)

    nw = nw_ref[...]
    dk = LANES
    masks = _gdn_block_masks(r, chunk, dk)

    for h in range(heads):
        ls = slice(h * LANES, (h + 1) * LANES)
        qh = qc[:, ls]
        kh = kc[:, ls]
        qh = qh * lax.rsqrt(jnp.sum(qh * qh, axis=-1, keepdims=True) + L2_EPS) * q_scale
        kh = kh * lax.rsqrt(jnp.sum(kh * kh, axis=-1, keepdims=True) + L2_EPS)
        q_eff, o0, m_mat, b_mat, gl = _gdn_block_terms(
            qh, kh, vc[:, ls], gc_ref[:, ls], gr_ref[h, 0], be_ref[:, ls], chunk, masks)
        s = s_scr[h]
        for c in range(r // chunk):
            rs = slice(c * chunk, (c + 1) * chunk)
            bs = slice(c * dk, (c + 1) * dk)
            lhs = jnp.concatenate([q_eff[rs], m_mat[bs]], axis=0).astype(BF16)
            ls_s = _dot(lhs, s.astype(BF16))
            o = ls_s[:chunk] + o0[rs]
            s_decay = jnp.broadcast_to(jnp.exp(gl[c * chunk:c * chunk + 1, :]), s.shape)
            s = s * s_decay - ls_s[chunk:] + b_mat[bs]
            o_ref[rs, ls] = _gated_head_norm(o, z_ref[rs, ls], nw).astype(o_ref.dtype)
        s_scr[h] = s
        s_out_ref[0, h] = s


def _gdn_prompt(proj, gcb, gc_t, beb, conv_w, norm_w, m_total, batch, seq, n_heads, rows_per_step,
                heads_per_step):
    hd = n_heads * LANES
    g = heads_per_step
    gw = g * LANES
    r = rows_per_step
    nt = seq // r
    ncb = hd // gw
    assert r % LANES == 0 and r % GDN_CHUNK == 0
    gc_rows = gc_t.reshape(LANES, (batch * seq) // r, 1, r)
    row = lambda b, hg, t: b * nt + t
    sect = lambda s: pl.BlockSpec((r, gw), lambda b, hg, t: (row(b, hg, t), s * ncb + hg))
    gate = pl.BlockSpec((r, gw), lambda b, hg, t: (row(b, hg, t), hg))
    wsp = lambda s: pl.BlockSpec((conv_w.shape[0], gw), lambda b, hg, t: (0, s * ncb + hg))
    body = functools.partial(_gdn_prompt_body, heads=g, chunk=GDN_CHUNK, q_scale=float(LANES) ** -0.5)
    return pl.pallas_call(
        body,
        grid=(batch, n_heads // g, nt),
        in_specs=[sect(0), sect(1), sect(2), sect(3), gate,
                  pl.BlockSpec((g, 1, 1, r), lambda b, hg, t: (hg, row(b, hg, t), 0, 0)),
                  gate, wsp(0), wsp(1), wsp(2),
                  pl.BlockSpec((1, LANES), lambda b, hg, t: (0, 0))],
        out_specs=[
            pl.BlockSpec((r, gw), lambda b, hg, t: (row(b, hg, t), hg)),
            pl.BlockSpec((1, g, LANES, LANES), lambda b, hg, t: (b, hg, 0, 0)),
        ],
        out_shape=[
            jax.ShapeDtypeStruct((m_total, hd), BF16),
            jax.ShapeDtypeStruct((batch, n_heads, LANES, LANES), F32),
        ],
        scratch_shapes=[pltpu.VMEM((g, LANES, LANES), F32), pltpu.VMEM((3, SUBLANES, gw), F32)],
        compiler_params=_params("parallel", "parallel", "arbitrary"),
        name="gdn_prompt",
    )(proj, proj, proj, proj, gcb, gc_rows, beb, conv_w, conv_w, conv_w, norm_w.reshape(1, LANES))


def _gdn_sample_body(q_ref, k_ref, v_ref, z_ref, sq_ref, sk_ref, sv_ref, g_ref, be_ref, wq_ref, wk_ref, wv_ref,
                     nw_ref, s_ref, *rest, q_scale):
    o_ref, s_out_ref, q_scr, k_scr, d_scr, o_scr = rest[-6:]
    nb = q_ref.shape[0]
    q = _silu(_conv_from_state(sq_ref, q_ref[...], wq_ref[...]))
    k = _silu(_conv_from_state(sk_ref, k_ref[...], wk_ref[...]))
    v = _silu(_conv_from_state(sv_ref, v_ref[...], wv_ref[...]))
    q_scr[...] = q * lax.rsqrt(jnp.sum(q * q, axis=-1, keepdims=True) + L2_EPS) * q_scale
    k_scr[...] = k * lax.rsqrt(jnp.sum(k * k, axis=-1, keepdims=True) + L2_EPS)
    d_scr[0] = jnp.exp(g_ref[...])
    d_scr[1] = be_ref[...]
    d_scr[2] = v

    def step(b, carry):
        row = pl.ds(b, 1)
        k_col = jnp.transpose(jnp.broadcast_to(k_scr[row, :], (LANES, LANES)))
        q_col = jnp.transpose(jnp.broadcast_to(q_scr[row, :], (LANES, LANES)))
        s = s_ref[b] * d_scr[0, row, :]
        ks = jnp.sum(k_col * s, axis=0, keepdims=True)
        delta = d_scr[1, row, :] * (d_scr[2, row, :] - ks)
        s = s + k_col * delta
        s_out_ref[b] = s
        o_scr[row, :] = jnp.sum(q_col * s, axis=0, keepdims=True)
        return carry

    lax.fori_loop(0, nb, step, 0)
    o_ref[...] = _gated_head_norm(o_scr[...], z_ref[...], nw_ref[...]).astype(o_ref.dtype)


def _gdn_sample(proj, conv_state, gb, beb, conv_w, norm_w, s_all, layer, o_full, s_new_all, row0, n_heads,
                samples_per_step):
    nsamp = s_all.shape[1]
    bs = samples_per_step
    assert row0 % bs == 0 and nsamp % bs == 0
    rb0 = row0 // bs
    nh = n_heads
    kc = conv_w.shape[0]
    sect = lambda s: pl.BlockSpec((bs, LANES), lambda h, i: (rb0 + i, s * nh + h))
    stat = lambda s: pl.BlockSpec((kc - 1, bs, LANES), lambda h, i: (0, i, s * nh + h))
    gate = pl.BlockSpec((bs, LANES), lambda h, i: (i, h))
    wsp = lambda s: pl.BlockSpec((kc, LANES), lambda h, i: (0, s * nh + h))
    sspec = pl.BlockSpec((None, bs, None, LANES, LANES), lambda h, i: (layer, i, h, 0, 0))
    untouched = pl.BlockSpec(memory_space=pl.ANY)
    in_specs = [sect(0), sect(1), sect(2), sect(3), stat(0), stat(1), stat(2), gate, gate,
                wsp(0), wsp(1), wsp(2), pl.BlockSpec((1, LANES), lambda h, i: (0, 0)), sspec, untouched]
    args = [proj, proj, proj, proj, conv_state, conv_state, conv_state, gb, beb, conv_w, conv_w, conv_w,
            norm_w.reshape(1, LANES), s_all, o_full]
    aliases = {len(args) - 1: 0}
    if s_new_all is not None:
        in_specs.append(untouched)
        args.append(s_new_all)
        aliases[len(args) - 1] = 1
    body = functools.partial(_gdn_sample_body, q_scale=float(LANES) ** -0.5)
    return pl.pallas_call(
        body,
        grid=(nh, nsamp // bs),
        in_specs=in_specs,
        out_specs=[pl.BlockSpec((bs, LANES), lambda h, i: (rb0 + i, h)), sspec],
        out_shape=[
            jax.ShapeDtypeStruct(o_full.shape, o_full.dtype),
            jax.ShapeDtypeStruct(s_all.shape, F32),
        ],
        input_output_aliases=aliases,
        scratch_shapes=[pltpu.VMEM((bs, LANES), F32), pltpu.VMEM((bs, LANES), F32),
                        pltpu.VMEM((3, bs, LANES), F32), pltpu.VMEM((bs, LANES), F32)],
        compiler_params=_params("parallel", "parallel"),
        name="gdn_sample",
    )(*args)


def _lru_coeffs(xc, wg_ref, bgr_ref, bgi_ref, lam_ref):
    bw = xc.shape[1]
    gates = _dot(xc.astype(BF16), wg_ref[0])
    r = _sigmoid(gates[:, :bw] + bgr_ref[...])
    i = _sigmoid(gates[:, bw:] + bgi_ref[...])
    log_a = -LRU_C * r * _softplus(-lam_ref[...])
    a = jnp.exp(log_a)
    mult = jnp.sqrt(-jnp.tanh(log_a) * (a * a + 1.0))
    return a, mult * (i * xc)


def _scan8(a, b):
    row = lax.broadcasted_iota(jnp.int32, a.shape, 0)
    for s in (1, 2, 4):
        a_sh = pltpu.roll(a, s, 0)
        b_sh = pltpu.roll(b, s, 0)
        m = row >= s
        b = jnp.where(m, a * b_sh + b, b)
        a = jnp.where(m, a * a_sh, a)
    return a, b


def _lru_prompt_body(x_ref, y_ref, cw_ref, cb_ref, wg_ref, bgr_ref, bgi_ref, lam_ref,
                     o_ref, h_out_ref, carry_scr, h_scr, a_scr, b_scr):
    t = pl.program_id(2)

    @pl.when(t == 0)
    def _():
        carry_scr[...] = jnp.zeros_like(carry_scr)
        h_scr[...] = jnp.zeros_like(h_scr)

    x = x_ref[...]
    r = x.shape[0]
    xc = _causal_conv_rows(x, carry_scr[...], cw_ref[...]) + cb_ref[...]
    carry_scr[...] = x[r - SUBLANES:]
    a, b = _lru_coeffs(xc, wg_ref, bgr_ref, bgi_ref, lam_ref)
    a_scr[...] = a
    b_scr[...] = b

    def group(i, h_prev):
        rows = pl.ds(pl.multiple_of(i * SUBLANES, SUBLANES), SUBLANES)
        a8, b8 = _scan8(a_scr[rows, :], b_scr[rows, :])
        h8 = a8 * h_prev + b8
        b_scr[rows, :] = h8
        return jnp.broadcast_to(h8[SUBLANES - 1:SUBLANES, :], h8.shape)

    h_last = lax.fori_loop(0, r // SUBLANES, group, h_scr[...])
    h_scr[...] = h_last
    h_out_ref[0] = h_last[0:1, :]
    o_ref[...] = (b_scr[...] * _gelu_tanh(y_ref[...])).astype(o_ref.dtype)


def _lru_prompt(proj, conv_w, conv_b, w_gate, b_gate, lam, m_total, batch, seq, rows_per_step):
    nb, bw = w_gate.shape[0], w_gate.shape[1]
    drnn = nb * bw
    r = rows_per_step
    nt = seq // r
    row = lambda b, n, t: b * nt + t
    vec = lambda off: pl.BlockSpec((1, bw), lambda b, n, t: (0, off + n))
    return pl.pallas_call(
        _lru_prompt_body,
        grid=(batch, nb, nt),
        in_specs=[
            pl.BlockSpec((r, bw), lambda b, n, t: (row(b, n, t), n)),
            pl.BlockSpec((r, bw), lambda b, n, t: (row(b, n, t), nb + n)),
            pl.BlockSpec((conv_w.shape[0], bw), lambda b, n, t: (0, n)),
            vec(0),
            pl.BlockSpec((1, bw, 2 * bw), lambda b, n, t: (n, 0, 0)),
            vec(0), vec(nb), vec(0),
        ],
        out_specs=[
            pl.BlockSpec((r, bw), lambda b, n, t: (row(b, n, t), n)),
            pl.BlockSpec((1, 1, bw), lambda b, n, t: (b, 0, n)),
        ],
        out_shape=[
            jax.ShapeDtypeStruct((m_total, drnn), BF16),
            jax.ShapeDtypeStruct((batch, 1, drnn), F32),
        ],
        scratch_shapes=[pltpu.VMEM((SUBLANES, bw), F32), pltpu.VMEM((SUBLANES, bw), F32),
                        pltpu.VMEM((r, bw), F32), pltpu.VMEM((r, bw), F32)],
        compiler_params=_params("parallel", "parallel", "arbitrary"),
        name="lru_prompt",
    )(proj, proj, conv_w, conv_b.reshape(1, drnn), w_gate, b_gate.reshape(1, 2 * drnn),
      b_gate.reshape(1, 2 * drnn), lam.reshape(1, drnn))


def _lru_sample_body(x_ref, y_ref, st_ref, h0_ref, cw_ref, cb_ref, wg_ref, bgr_ref, bgi_ref, lam_ref,
                     o_full_ref, o_ref, h_out_ref):
    del o_full_ref
    xc = _conv_from_state(st_ref, x_ref[...], cw_ref[...]) + cb_ref[...]
    a, b = _lru_coeffs(xc, wg_ref, bgr_ref, bgi_ref, lam_ref)
    h = a * h0_ref[...] + b
    h_out_ref[...] = h
    o_ref[...] = (h * _gelu_tanh(y_ref[...])).astype(o_ref.dtype)


def _lru_sample(proj, conv_state, h0, conv_w, conv_b, w_gate, b_gate, lam, o_full, row0):
    nsamp = h0.shape[0]
    nb, bw = w_gate.shape[0], w_gate.shape[1]
    drnn = nb * bw
    kc = conv_w.shape[0]
    assert row0 % nsamp == 0
    rb0 = row0 // nsamp
    vec = lambda off: pl.BlockSpec((1, bw), lambda n: (0, off + n))
    return pl.pallas_call(
        _lru_sample_body,
        grid=(nb,),
        in_specs=[
            pl.BlockSpec((nsamp, bw), lambda n: (rb0, n)),
            pl.BlockSpec((nsamp, bw), lambda n: (rb0, nb + n)),
            pl.BlockSpec((kc - 1, nsamp, bw), lambda n: (0, 0, n)),
            pl.BlockSpec((nsamp, bw), lambda n: (0, n)),
            pl.BlockSpec((kc, bw), lambda n: (0, n)),
            vec(0),
            pl.BlockSpec((1, bw, 2 * bw), lambda n: (n, 0, 0)),
            vec(0), vec(nb), vec(0),
            pl.BlockSpec(memory_space=pl.ANY),
        ],
        out_specs=[pl.BlockSpec((nsamp, bw), lambda n: (rb0, n)), pl.BlockSpec((nsamp, bw), lambda n: (0, n))],
        out_shape=[jax.ShapeDtypeStruct(o_full.shape, o_full.dtype), jax.ShapeDtypeStruct((nsamp, drnn), F32)],
        input_output_aliases={10: 0},
        compiler_params=_params("parallel"),
        name="lru_sample",
    )(proj, proj, conv_state, h0, conv_w, conv_b.reshape(1, drnn), w_gate, b_gate.reshape(1, 2 * drnn),
      b_gate.reshape(1, 2 * drnn), lam.reshape(1, drnn), o_full)


def _ffn_prompt_body(g_ref, v_ref, cw_ref, cb_ref, o_ref, carry_scr):
    t = pl.program_id(2)

    @pl.when(t == 0)
    def _():
        carry_scr[...] = jnp.zeros_like(carry_scr)

    g = g_ref[...]
    r = g.shape[0]
    gate = _causal_conv_rows(g, carry_scr[...], cw_ref[...]) + cb_ref[...]
    carry_scr[...] = g[r - SUBLANES:]
    o_ref[...] = (_gelu_erf(gate) * v_ref[...]).astype(o_ref.dtype)


def _ffn_prompt(proj, conv_w, conv_b, m_total, batch, seq, rows_per_step, cols_per_step):
    dff = conv_w.shape[1]
    r, tc = rows_per_step, cols_per_step
    nt, nc = seq // r, dff // tc
    row = lambda b, c, t: b * nt + t
    return pl.pallas_call(
        _ffn_prompt_body,
        grid=(batch, nc, nt),
        in_specs=[
            pl.BlockSpec((r, tc), lambda b, c, t: (row(b, c, t), c)),
            pl.BlockSpec((r, tc), lambda b, c, t: (row(b, c, t), nc + c)),
            pl.BlockSpec((conv_w.shape[0], tc), lambda b, c, t: (0, c)),
            pl.BlockSpec((1, tc), lambda b, c, t: (0, c)),
        ],
        out_specs=pl.BlockSpec((r, tc), lambda b, c, t: (row(b, c, t), c)),
        out_shape=jax.ShapeDtypeStruct((m_total, dff), BF16),
        scratch_shapes=[pltpu.VMEM((SUBLANES, tc), F32)],
        compiler_params=_params("parallel", "parallel", "arbitrary"),
        name="ffn_prompt",
    )(proj, proj, conv_w, conv_b.reshape(1, dff))


def _ffn_sample_body(g_ref, v_ref, st_ref, cw_ref, cb_ref, o_full_ref, o_ref):
    del o_full_ref
    gate = _conv_from_state(st_ref, g_ref[...], cw_ref[...]) + cb_ref[...]
    o_ref[...] = (_gelu_erf(gate) * v_ref[...]).astype(o_ref.dtype)


def _ffn_sample(proj, conv_state, conv_w, conv_b, o_full, row0, cols_per_step):
    nsamp = conv_state.shape[1]
    dff = conv_w.shape[1]
    kc = conv_w.shape[0]
    tc = cols_per_step
    nc = dff // tc
    assert row0 % nsamp == 0
    rb0 = row0 // nsamp
    return pl.pallas_call(
        _ffn_sample_body,
        grid=(nc,),
        in_specs=[
            pl.BlockSpec((nsamp, tc), lambda c: (rb0, c)),
            pl.BlockSpec((nsamp, tc), lambda c: (rb0, nc + c)),
            pl.BlockSpec((kc - 1, nsamp, tc), lambda c: (0, 0, c)),
            pl.BlockSpec((kc, tc), lambda c: (0, c)),
            pl.BlockSpec((1, tc), lambda c: (0, c)),
            pl.BlockSpec(memory_space=pl.ANY),
        ],
        out_specs=pl.BlockSpec((nsamp, tc), lambda c: (rb0, c)),
        out_shape=jax.ShapeDtypeStruct(o_full.shape, o_full.dtype),
        input_output_aliases={5: 0},
        compiler_params=_params("parallel"),
        name="ffn_sample",
    )(proj, proj, conv_state, conv_w, conv_b.reshape(1, dff), o_full)


def _new_conv_state(old_state, pre_rows):
    return jnp.concatenate([old_state[:, 1:], pre_rows[:, None, :]], axis=1)


def _prompt_conv_tail(proj, batch, seq, keep, cols):
    return jnp.stack([lax.slice(proj, ((b + 1) * seq - keep, 0), ((b + 1) * seq, cols)) for b in range(batch)])


def _sample_rows(proj, mp, cols):
    return lax.slice(proj, (mp, 0), (proj.shape[0], cols))


def _gdn_layer(x, h, mp, batch, seq, j, s_all, s_new_all, conv_state, w_in_bf, w_in, conv_w, a_log, dt_bias,
               norm_w, w_out_bf):
    n_heads = a_log.shape[0]
    m_total, d_model = x.shape
    vd = w_out_bf.shape[1]
    cd = conv_w.shape[1]
    nsamp = s_all.shape[1]
    assert vd == n_heads * LANES and cd == 3 * vd, "kernels assume DK == DV == 128"
    kc = conv_w.shape[0]

    w_ab = jnp.zeros((d_model, 2 * LANES), F32)
    w_ab = w_ab.at[:, :n_heads].set(w_in[:, cd + vd:cd + vd + n_heads])
    w_ab = w_ab.at[:, LANES:LANES + n_heads].set(w_in[:, cd + vd + n_heads:])
    pad = lambda p: jnp.zeros((1, LANES), F32).at[0, :n_heads].set(p.astype(F32))
    expand = (jnp.arange(LANES)[:, None] == (jnp.arange(vd) // LANES)[None, :]).astype(BF16)

    proj = _matmul(h, w_in_bf, j, n_cols=cd + vd)
    ab = _matmul(h, w_ab.astype(BF16)[None], 0)

    gcb, beb, gc_t = _gdn_gates(ab, pad(a_log), pad(dt_bias), expand, 0, mp, True)
    o, s_p = _gdn_prompt(proj, gcb, gc_t, beb, conv_w, norm_w, m_total, batch, seq, n_heads,
                         rows_per_step=min(seq, 256), heads_per_step=min(n_heads, 4))

    gb_s, beb_s = _gdn_gates(ab, pad(a_log), pad(dt_bias), expand, mp, nsamp, False)
    state_t = jnp.transpose(conv_state, (1, 0, 2))
    o, s_new_all = _gdn_sample(proj, state_t, gb_s, beb_s, conv_w, norm_w, s_all, j, o, s_new_all, mp,
                               n_heads, samples_per_step=min(nsamp, 64))

    x = _matmul(o, w_out_bf, j, residual=x)

    conv_p = _prompt_conv_tail(proj, batch, seq, kc - 1, cd)
    conv_s = _new_conv_state(conv_state, _sample_rows(proj, mp, cd))
    return x, s_p, s_new_all, conv_p, conv_s


def _lru_layer(x, h, mp, batch, seq, j, h0, conv_state, w_in_bf, b_in, conv_w, conv_b, w_gate_bf, b_gate, lam,
               w_out_bf):
    m_total = x.shape[0]
    drnn = conv_w.shape[1]
    kc = conv_w.shape[0]
    proj = _matmul(h, w_in_bf, j, bias=b_in)
    o, h_p = _lru_prompt(proj, conv_w, conv_b, w_gate_bf, b_gate, lam, m_total, batch, seq,
                         rows_per_step=min(seq, 512))
    state_t = jnp.transpose(conv_state, (1, 0, 2))
    o, h_s = _lru_sample(proj, state_t, h0, conv_w, conv_b, w_gate_bf, b_gate, lam, o, mp)
    x = _matmul(o, w_out_bf, j, residual=x)
    conv_p = _prompt_conv_tail(proj, batch, seq, kc - 1, drnn)
    conv_s = _new_conv_state(conv_state, _sample_rows(proj, mp, drnn))
    return x, h_p.reshape(batch, drnn), h_s, conv_p, conv_s


def _ffn_layer(x, h, mp, batch, seq, i, conv_state, w_in_bf, conv_w, conv_b, w_out_bf):
    m_total = x.shape[0]
    dff = conv_w.shape[1]
    kc = conv_w.shape[0]
    proj = _matmul(h, w_in_bf, i, tn=512)
    tc = _pick(dff, 5504, LANES)
    act = _ffn_prompt(proj, conv_w, conv_b, m_total, batch, seq, rows_per_step=min(seq, 128), cols_per_step=tc)
    state_t = jnp.transpose(conv_state, (1, 0, 2))
    act = _ffn_sample(proj, state_t, conv_w, conv_b, act, mp, tc)
    x = _matmul(act, w_out_bf, i, residual=x, tn=512, tk=_pick(dff, 5504, LANES))
    conv_p = _prompt_conv_tail(proj, batch, seq, kc - 1, dff)
    conv_s = _new_conv_state(conv_state, _sample_rows(proj, mp, dff))
    return x, conv_p, conv_s


def kernel(x_prompt, x_sample, state_gdn_S, state_gdn_conv, state_lru_h, state_lru_conv, state_ffn_conv, norm_mixer, norm_ffn, norm_final, gdn_w_in, gdn_conv_w, gdn_A_log, gdn_dt_bias, gdn_norm_w, gdn_w_out, lru_w_in, lru_b_in, lru_conv_w, lru_conv_b, lru_w_gate, lru_b_gate, lru_lambda, lru_w_out, ffn_w_in, ffn_conv_w, ffn_conv_b, ffn_w_out):
    batch, seq, d_model = x_prompt.shape
    nsamp = x_sample.shape[0]
    depth = norm_mixer.shape[0]
    mp = batch * seq
    x = jnp.concatenate([x_prompt.reshape(mp, d_model), x_sample.reshape(nsamp, d_model)], axis=0)

    gdn_w_in_bf, gdn_w_out_bf = gdn_w_in.astype(BF16), gdn_w_out.astype(BF16)
    lru_w_in_bf, lru_w_out_bf = lru_w_in.astype(BF16), lru_w_out.astype(BF16)
    lru_w_gate_bf = lru_w_gate.astype(BF16)
    ffn_w_in_bf, ffn_w_out_bf = ffn_w_in.astype(BF16), ffn_w_out.astype(BF16)

    gs_p, gc_p, gc_s = [], [], []
    gs_s = None
    lh_p, lh_s, lc_p, lc_s = [], [], [], []
    fc_p, fc_s = [], []
    for i in range(depth):
        j = i // 2
        h = _rmsnorm(x, norm_mixer[i], BF16)
        if i % 2 == 0:
            x, sp, gs_s, cp, cs = _gdn_layer(
                x, h, mp, batch, seq, j, state_gdn_S, gs_s, state_gdn_conv[j], gdn_w_in_bf, gdn_w_in[j],
                gdn_conv_w[j], gdn_A_log[j], gdn_dt_bias[j], gdn_norm_w[j], gdn_w_out_bf)
            gs_p.append(sp); gc_p.append(cp); gc_s.append(cs)
        else:
            x, hp, hs, cp, cs = _lru_layer(
                x, h, mp, batch, seq, j, state_lru_h[j], state_lru_conv[j], lru_w_in_bf, lru_b_in[j],
                lru_conv_w[j], lru_conv_b[j], lru_w_gate_bf[j], lru_b_gate[j], lru_lambda[j], lru_w_out_bf)
            lh_p.append(hp); lh_s.append(hs); lc_p.append(cp); lc_s.append(cs)
        h = _rmsnorm(x, norm_ffn[i], BF16)
        x, cp, cs = _ffn_layer(x, h, mp, batch, seq, i, state_ffn_conv[i], ffn_w_in_bf, ffn_conv_w[i],
                               ffn_conv_b[i], ffn_w_out_bf)
        fc_p.append(cp); fc_s.append(cs)

    y_prompt = _rmsnorm(x, norm_final, F32, 0, mp).reshape(batch, seq, d_model)
    y_sample = _rmsnorm(x, norm_final, F32, mp, nsamp).reshape(nsamp, 1, d_model)
    return (y_prompt, y_sample,
            jnp.stack(gs_p), gs_s, jnp.stack(gc_p), jnp.stack(gc_s),
            jnp.stack(lh_p), jnp.stack(lh_s), jnp.stack(lc_p), jnp.stack(lc_s),
            jnp.stack(fc_p), jnp.stack(fc_s))
```

```python
import functools
import math

import jax
import jax.numpy as jnp
from jax import lax
from jax.experimental import pallas as pl
from jax.experimental.pallas import tpu as pltpu

F32 = jnp.float32
BF16 = jnp.bfloat16
HIGHEST = lax.Precision.HIGHEST

NORM_EPS = 1e-6
L2_EPS = 1e-6
LRU_C = 8.0
GDN_CHUNK = 64

LANES = 128
SUBLANES = 8
VMEM_LIMIT_BYTES = 56 * 1024 * 1024


def _params(*sem):
    return pltpu.CompilerParams(dimension_semantics=sem, vmem_limit_bytes=VMEM_LIMIT_BYTES)


def _pick(n, target, align):
    best = None
    t = align
    while t <= min(n, target):
        if n % t == 0:
            best = t
        t += align
    return best if best is not None else n


def _dot(a, b, precision=None):
    return jnp.dot(a, b, preferred_element_type=F32, precision=precision)


def _dot_nt(a, b, precision=None):
    return lax.dot_general(a, b, (((1,), (1,)), ((), ())), preferred_element_type=F32, precision=precision)


def _softplus(x):
    return jnp.maximum(x, 0.0) + jnp.log1p(jnp.exp(-jnp.abs(x)))


def _sigmoid(x):
    return 1.0 / (1.0 + jnp.exp(-x))


def _silu(x):
    return x * _sigmoid(x)


def _gelu_tanh(x):
    c = math.sqrt(2.0 / math.pi)
    return 0.5 * x * (1.0 + jnp.tanh(c * (x + 0.044715 * (x * x * x))))


def _gelu_erf(x):
    return 0.5 * x * (1.0 + lax.erf(x * (1.0 / math.sqrt(2.0))))


def _rmsnorm_body(x_ref, w_ref, o_ref):
    x = x_ref[...]
    y = x * lax.rsqrt(jnp.mean(x * x, axis=-1, keepdims=True) + NORM_EPS)
    o_ref[...] = (y * w_ref[...]).astype(o_ref.dtype)


def _rmsnorm(x, w, out_dtype, row0=0, rows=None):
    d = x.shape[1]
    rows = x.shape[0] if rows is None else rows
    tr = _pick(math.gcd(rows, row0) if row0 else rows, 512, 16)
    rb0 = row0 // tr
    return pl.pallas_call(
        _rmsnorm_body,
        grid=(rows // tr,),
        in_specs=[pl.BlockSpec((tr, d), lambda i: (rb0 + i, 0)), pl.BlockSpec((1, d), lambda i: (0, 0))],
        out_specs=pl.BlockSpec((tr, d), lambda i: (i, 0)),
        out_shape=jax.ShapeDtypeStruct((rows, d), out_dtype),
        compiler_params=_params("parallel"),
        name="rmsnorm",
    )(x, w.reshape(1, d))


def _matmul_body(*refs, nk, has_bias, has_res, cast_w):
    x_ref, w_ref = refs[0], refs[1]
    pos = 2
    b_ref = r_ref = None
    if has_bias:
        b_ref = refs[pos]
        pos += 1
    if has_res:
        r_ref = refs[pos]
        pos += 1
    o_ref = refs[pos]

    def finish(acc):
        if has_bias:
            acc = acc + b_ref[...]
        if has_res:
            acc = r_ref[...] + acc
        o_ref[...] = acc.astype(o_ref.dtype)

    if cast_w:
        wb_ref = refs[pos + 1]

        @pl.when(pl.program_id(1) == 0)
        def _():
            wb_ref[...] = w_ref[...].astype(BF16)

        finish(_dot(x_ref[...], wb_ref[...]))
        return

    part = _dot(x_ref[...], w_ref[...])
    if nk == 1:
        finish(part)
    else:
        acc_ref = refs[pos + 1]
        k = pl.program_id(2)

        @pl.when(k == 0)
        def _():
            acc_ref[...] = part

        @pl.when(k > 0)
        def _():
            acc_ref[...] += part

        @pl.when(k == nk - 1)
        def _():
            finish(acc_ref[...])


def _matmul(x, w, layer, bias=None, residual=None, n_cols=None, out_dtype=F32, tm=832, tn=1024, tk=4096,
            row0=0, rows=None):
    kd = x.shape[1]
    m = x.shape[0] if rows is None else rows
    n = w.shape[2] if n_cols is None else n_cols
    cast_w = w.dtype != BF16
    tm = _pick(math.gcd(m, row0) if row0 else m, tm, 16)
    tn = _pick(n, tn, LANES)
    tk = kd if cast_w else _pick(kd, tk, LANES)
    nk = kd // tk
    rb0 = row0 // tm
    assert residual is None or row0 == 0
    in_specs = [pl.BlockSpec((tm, tk), lambda j, i, k: (rb0 + i, k)),
                pl.BlockSpec((None, tk, tn), lambda j, i, k: (layer, k, j))]
    args = [x, w]
    if bias is not None:
        in_specs.append(pl.BlockSpec((1, tn), lambda j, i, k: (0, j)))
        args.append(bias.reshape(1, n).astype(F32))
    if residual is not None:
        in_specs.append(pl.BlockSpec((tm, tn), lambda j, i, k: (i, j)))
        args.append(residual)
    body = functools.partial(_matmul_body, nk=nk, has_bias=bias is not None, has_res=residual is not None,
                             cast_w=cast_w)
    if cast_w:
        scratch = [pltpu.VMEM((tk, tn), BF16)]
    else:
        scratch = [pltpu.VMEM((tm, tn), F32)] if nk > 1 else []
    return pl.pallas_call(
        body,
        grid=(n // tn, m // tm, nk),
        in_specs=in_specs,
        out_specs=pl.BlockSpec((tm, tn), lambda j, i, k: (i, j)),
        out_shape=jax.ShapeDtypeStruct((m, n), out_dtype),
        scratch_shapes=scratch,
        compiler_params=_params("parallel", "arbitrary", "arbitrary"),
        name="matmul",
    )(*args)


def _shift_rows(x, prev8, s):
    if s == 0:
        return x
    xs = pltpu.roll(x, s, 0)
    row = lax.broadcasted_iota(jnp.int32, prev8.shape, 0)
    top = jnp.where(row < s, pltpu.roll(prev8, s, 0), xs[:SUBLANES])
    if x.shape[0] == SUBLANES:
        return top
    return jnp.concatenate([top, xs[SUBLANES:]], axis=0)


def _causal_conv_rows(x, prev8, w):
    kc = w.shape[0]
    y = None
    for kk in range(kc):
        term = _shift_rows(x, prev8, kc - 1 - kk) * w[kk:kk + 1]
        y = term if y is None else y + term
    return y


def _conv_from_state(state_ref, x, w):
    kc = w.shape[0]
    y = state_ref[0] * w[0:1]
    for kk in range(1, kc - 1):
        y = y + state_ref[kk] * w[kk:kk + 1]
    return y + x * w[kc - 1:kc]


def _expand_lanes(x, e):
    hi = x.astype(BF16)
    rest = x - hi.astype(F32)
    mid = rest.astype(BF16)
    lo = (rest - mid.astype(F32)).astype(BF16)
    return _dot(hi, e) + _dot(mid, e) + _dot(lo, e)


def _gdn_gate_body(a_ref, b_ref, alog_ref, dtb_ref, e_ref, *out_refs, chunk, cumulative):
    g = -jnp.exp(alog_ref[...]) * _softplus(a_ref[...] + dtb_ref[...])
    beta = _sigmoid(b_ref[...])
    e = e_ref[...]
    if cumulative:
        r = g.shape[0]
        ri = lax.broadcasted_iota(jnp.int32, (chunk, chunk), 0)
        ci = lax.broadcasted_iota(jnp.int32, (chunk, chunk), 1)
        tril = (ri >= ci).astype(F32)
        gc = jnp.concatenate(
            [_dot(tril, g[c * chunk:(c + 1) * chunk], HIGHEST) for c in range(r // chunk)], axis=0)
        out_refs[0][...] = _expand_lanes(gc, e)
        out_refs[1][...] = _expand_lanes(beta, e)
        out_refs[2][...] = jnp.transpose(gc)
    else:
        out_refs[0][...] = _expand_lanes(g, e)
        out_refs[1][...] = _expand_lanes(beta, e)


def _gdn_gates(ab, alog_pad, dtb_pad, expand, row0, rows, cumulative):
    hd = expand.shape[1]
    tr = _pick(rows, 256, LANES)
    cw = _pick(hd, 1024, LANES)
    rb0 = row0 // tr
    assert row0 % tr == 0 and tr % GDN_CHUNK == 0
    wide = pl.BlockSpec((tr, cw), lambda i, j: (i, j))
    out_specs = [wide, wide]
    out_shape = [jax.ShapeDtypeStruct((rows, hd), F32)] * 2
    if cumulative:
        out_specs.append(pl.BlockSpec((LANES, tr), lambda i, j: (0, i)))
        out_shape.append(jax.ShapeDtypeStruct((LANES, rows), F32))
    body = functools.partial(_gdn_gate_body, chunk=GDN_CHUNK, cumulative=cumulative)
    return pl.pallas_call(
        body,
        grid=(rows // tr, hd // cw),
        in_specs=[
            pl.BlockSpec((tr, LANES), lambda i, j: (rb0 + i, 0)),
            pl.BlockSpec((tr, LANES), lambda i, j: (rb0 + i, 1)),
            pl.BlockSpec((1, LANES), lambda i, j: (0, 0)),
            pl.BlockSpec((1, LANES), lambda i, j: (0, 0)),
            pl.BlockSpec((LANES, cw), lambda i, j: (0, j)),
        ],
        out_specs=out_specs,
        out_shape=out_shape,
        compiler_params=_params("parallel", "arbitrary"),
        name="gdn_gates",
    )(ab, ab, alog_pad, dtb_pad, expand)


def _gated_head_norm(o, z, nw):
    o = o * lax.rsqrt(jnp.mean(o * o, axis=-1, keepdims=True) + NORM_EPS) * nw
    return o * _silu(z)


def _gdn_block_masks(r, chunk, dk):
    ri = lax.broadcasted_iota(jnp.int32, (r, r), 0)
    ci = lax.broadcasted_iota(jnp.int32, (r, r), 1)
    first = (ri // chunk) * chunk
    causal = (ci >= first) & (ci <= ri)
    strict = (ci >= first) & (ci < ri)
    eye = (ri == ci).astype(F32)
    bi = lax.broadcasted_iota(jnp.int32, (r // chunk * dk, r), 0) // dk
    bj = lax.broadcasted_iota(jnp.int32, (r // chunk * dk, r), 1) // chunk
    return causal, strict, eye, bi == bj


def _gdn_block_terms(qs, ks, vs, gcbs, gc_rows, bebs, chunk, masks):
    nh = len(qs)
    hs = range(nh)
    r = ks[0].shape[0]
    dv = vs[0].shape[1]
    nch = r // chunk
    causal, strict, eye, kt_mask = masks
    decay = [jnp.where(causal, jnp.exp(jnp.where(
        causal, jnp.concatenate([gcbs[h]] * (r // LANES), axis=1) - gc_rows[h], 0.0)), 0.0) for h in hs]
    eg = [jnp.exp(gcbs[h]) for h in hs]
    kb = [ks[h] * bebs[h] for h in hs]
    kq = [_dot_nt(jnp.concatenate([kb[h], qs[h]], axis=0).astype(BF16), ks[h].astype(BF16)) for h in hs]
    qk = [kq[h][r:] * decay[h] for h in hs]
    n = [-jnp.where(strict, kq[h][:r] * decay[h], 0.0) for h in hs]
    tm = [eye + n[h] for h in hs]
    nb = [n[h].astype(BF16) for h in hs]
    p = [_dot(nb[h], nb[h]) for h in hs]
    steps = int(math.log2(chunk)) - 1
    for i in range(steps):
        pb = [p[h].astype(BF16) for h in hs]
        if i < steps - 1:
            both = [_dot(jnp.concatenate([pb[h], tm[h].astype(BF16)], axis=0), pb[h]) for h in hs]
            p = [both[h][:r] for h in hs]
            tm = [tm[h] + both[h][r:] for h in hs]
        else:
            tm = [tm[h] + _dot(tm[h].astype(BF16), pb[h]) for h in hs]
    uwb = [_dot(tm[h].astype(BF16), jnp.concatenate([vs[h] * bebs[h], kb[h] * eg[h]], axis=1).astype(BF16)
                ).astype(BF16) for h in hs]
    gl = [jnp.concatenate(
        [jnp.broadcast_to(gcbs[h][(c + 1) * chunk - 1:(c + 1) * chunk, :], (chunk, LANES)) for c in range(nch)],
        axis=0) for h in hs]
    k_tail_t = [jnp.transpose(ks[h] * jnp.exp(gl[h] - gcbs[h])) for h in hs]
    kt_blocks = [jnp.where(kt_mask, jnp.concatenate([k_tail_t[h]] * nch, axis=0), 0.0) for h in hs]
    both = [_dot(jnp.concatenate([qk[h], kt_blocks[h]], axis=0).astype(BF16), uwb[h]) for h in hs]
    return [(qs[h] * eg[h] - both[h][:r, dv:], both[h][:r, :dv], both[h][r:, dv:], both[h][r:, :dv], gl[h])
            for h in hs]


def _gdn_prompt_body(q_ref, k_ref, v_ref, z_ref, gc_ref, gr_ref, be_ref, wq_ref, wk_ref, wv_ref, nw_ref,
                     o_ref, s_out_ref, s_scr, carry_scr, *, heads, chunk, q_scale):
    t = pl.program_id(2)

    @pl.when(t == 0)
    def _():
        s_scr[...] = jnp.zeros_like(s_scr)
        carry_scr[...] = jnp.zeros_like(carry_scr)

    r = q_ref.shape[0]

    def conv_silu(x_ref, w_ref, idx):
        x = x_ref[...]
        y = _causal_conv_rows(x, carry_scr[idx], w_ref[...])
        carry_scr[idx] = x[r - SUBLANES:]
        return _silu(y)

    qc = conv_silu(q_ref, wq_ref, 0)
    kc = conv_silu(k_ref, wk_ref, 1)
    vc = conv_silu(v_ref, wv_ref, 2)

    nw = nw_ref[...]
    dk = LANES
    masks = _gdn_block_masks(r, chunk, dk)

    lanes = [slice(h * LANES, (h + 1) * LANES) for h in range(heads)]
    qs, ks = [], []
    for ls in lanes:
        qh = qc[:, ls]
        kh = kc[:, ls]
        qs.append(qh * lax.rsqrt(jnp.sum(qh * qh, axis=-1, keepdims=True) + L2_EPS) * q_scale)
        ks.append(kh * lax.rsqrt(jnp.sum(kh * kh, axis=-1, keepdims=True) + L2_EPS))
    terms = _gdn_block_terms(qs, ks, [vc[:, ls] for ls in lanes], [gc_ref[:, ls] for ls in lanes],
                             [gr_ref[h, 0] for h in range(heads)], [be_ref[:, ls] for ls in lanes], chunk, masks)
    s = [s_scr[h] for h in range(heads)]
    for c in range(r // chunk):
        rs = slice(c * chunk, (c + 1) * chunk)
        bs = slice(c * dk, (c + 1) * dk)
        for h, ls in enumerate(lanes):
            q_eff, o0, m_mat, b_mat, gl = terms[h]
            lhs = jnp.concatenate([q_eff[rs], m_mat[bs]], axis=0).astype(BF16)
            ls_s = _dot(lhs, s[h].astype(BF16))
            o = ls_s[:chunk] + o0[rs]
            s_decay = jnp.broadcast_to(jnp.exp(gl[c * chunk:c * chunk + 1, :]), s[h].shape)
            s[h] = s[h] * s_decay - ls_s[chunk:] + b_mat[bs]
            o_ref[rs, ls] = _gated_head_norm(o, z_ref[rs, ls], nw).astype(o_ref.dtype)
    for h in range(heads):
        s_scr[h] = s[h]
        s_out_ref[0, h] = s[h]


def _gdn_prompt(proj, gcb, gc_t, beb, conv_w, norm_w, m_total, batch, seq, n_heads, rows_per_step,
                heads_per_step):
    hd = n_heads * LANES
    g = heads_per_step
    gw = g * LANES
    r = rows_per_step
    nt = seq // r
    ncb = hd // gw
    assert r % LANES == 0 and r % GDN_CHUNK == 0
    gc_rows = gc_t.reshape(LANES, (batch * seq) // r, 1, r)
    row = lambda b, hg, t: b * nt + t
    sect = lambda s: pl.BlockSpec((r, gw), lambda b, hg, t: (row(b, hg, t), s * ncb + hg))
    gate = pl.BlockSpec((r, gw), lambda b, hg, t: (row(b, hg, t), hg))
    wsp = lambda s: pl.BlockSpec((conv_w.shape[0], gw), lambda b, hg, t: (0, s * ncb + hg))
    body = functools.partial(_gdn_prompt_body, heads=g, chunk=GDN_CHUNK, q_scale=float(LANES) ** -0.5)
    return pl.pallas_call(
        body,
        grid=(batch, n_heads // g, nt),
        in_specs=[sect(0), sect(1), sect(2), sect(3), gate,
                  pl.BlockSpec((g, 1, 1, r), lambda b, hg, t: (hg, row(b, hg, t), 0, 0)),
                  gate, wsp(0), wsp(1), wsp(2),
                  pl.BlockSpec((1, LANES), lambda b, hg, t: (0, 0))],
        out_specs=[
            pl.BlockSpec((r, gw), lambda b, hg, t: (row(b, hg, t), hg)),
            pl.BlockSpec((1, g, LANES, LANES), lambda b, hg, t: (b, hg, 0, 0)),
        ],
        out_shape=[
            jax.ShapeDtypeStruct((m_total, hd), BF16),
            jax.ShapeDtypeStruct((batch, n_heads, LANES, LANES), F32),
        ],
        scratch_shapes=[pltpu.VMEM((g, LANES, LANES), F32), pltpu.VMEM((3, SUBLANES, gw), F32)],
        compiler_params=_params("parallel", "parallel", "arbitrary"),
        name="gdn_prompt",
    )(proj, proj, proj, proj, gcb, gc_rows, beb, conv_w, conv_w, conv_w, norm_w.reshape(1, LANES))


def _gdn_sample_body(q_ref, k_ref, v_ref, z_ref, sq_ref, sk_ref, sv_ref, g_ref, be_ref, wq_ref, wk_ref, wv_ref,
                     nw_ref, s_ref, *rest, q_scale):
    o_ref, s_out_ref, q_scr, k_scr, d_scr, o_scr = rest[-6:]
    nb = q_ref.shape[0]
    q = _silu(_conv_from_state(sq_ref, q_ref[...], wq_ref[...]))
    k = _silu(_conv_from_state(sk_ref, k_ref[...], wk_ref[...]))
    v = _silu(_conv_from_state(sv_ref, v_ref[...], wv_ref[...]))
    q_scr[...] = q * lax.rsqrt(jnp.sum(q * q, axis=-1, keepdims=True) + L2_EPS) * q_scale
    k_scr[...] = k * lax.rsqrt(jnp.sum(k * k, axis=-1, keepdims=True) + L2_EPS)
    d_scr[0] = jnp.exp(g_ref[...])
    d_scr[1] = be_ref[...]
    d_scr[2] = v

    def step(b, carry):
        row = pl.ds(b, 1)
        k_col = jnp.transpose(jnp.broadcast_to(k_scr[row, :], (LANES, LANES)))
        q_col = jnp.transpose(jnp.broadcast_to(q_scr[row, :], (LANES, LANES)))
        s = s_ref[b] * d_scr[0, row, :]
        ks = jnp.sum(k_col * s, axis=0, keepdims=True)
        delta = d_scr[1, row, :] * (d_scr[2, row, :] - ks)
        s = s + k_col * delta
        s_out_ref[b] = s
        o_scr[row, :] = jnp.sum(q_col * s, axis=0, keepdims=True)
        return carry

    lax.fori_loop(0, nb, step, 0, unroll=4)
    o_ref[...] = _gated_head_norm(o_scr[...], z_ref[...], nw_ref[...]).astype(o_ref.dtype)


def _gdn_sample(proj, conv_state, gb, beb, conv_w, norm_w, s_all, layer, o_full, s_new_all, row0, n_heads,
                samples_per_step):
    nsamp = s_all.shape[1]
    bs = samples_per_step
    assert row0 % bs == 0 and nsamp % bs == 0
    rb0 = row0 // bs
    nh = n_heads
    kc = conv_w.shape[0]
    sect = lambda s: pl.BlockSpec((bs, LANES), lambda h, i: (rb0 + i, s * nh + h))
    stat = lambda s: pl.BlockSpec((kc - 1, bs, LANES), lambda h, i: (0, i, s * nh + h))
    gate = pl.BlockSpec((bs, LANES), lambda h, i: (i, h))
    wsp = lambda s: pl.BlockSpec((kc, LANES), lambda h, i: (0, s * nh + h))
    sspec = pl.BlockSpec((None, bs, None, LANES, LANES), lambda h, i: (layer, i, h, 0, 0))
    untouched = pl.BlockSpec(memory_space=pl.ANY)
    in_specs = [sect(0), sect(1), sect(2), sect(3), stat(0), stat(1), stat(2), gate, gate,
                wsp(0), wsp(1), wsp(2), pl.BlockSpec((1, LANES), lambda h, i: (0, 0)), sspec, untouched]
    args = [proj, proj, proj, proj, conv_state, conv_state, conv_state, gb, beb, conv_w, conv_w, conv_w,
            norm_w.reshape(1, LANES), s_all, o_full]
    aliases = {len(args) - 1: 0}
    if s_new_all is not None:
        in_specs.append(untouched)
        args.append(s_new_all)
        aliases[len(args) - 1] = 1
    body = functools.partial(_gdn_sample_body, q_scale=float(LANES) ** -0.5)
    return pl.pallas_call(
        body,
        grid=(nh, nsamp // bs),
        in_specs=in_specs,
        out_specs=[pl.BlockSpec((bs, LANES), lambda h, i: (rb0 + i, h)), sspec],
        out_shape=[
            jax.ShapeDtypeStruct(o_full.shape, o_full.dtype),
            jax.ShapeDtypeStruct(s_all.shape, F32),
        ],
        input_output_aliases=aliases,
        scratch_shapes=[pltpu.VMEM((bs, LANES), F32), pltpu.VMEM((bs, LANES), F32),
                        pltpu.VMEM((3, bs, LANES), F32), pltpu.VMEM((bs, LANES), F32)],
        compiler_params=_params("parallel", "parallel"),
        name="gdn_sample",
    )(*args)


def _lru_coeffs(xc, wg_ref, bgr_ref, bgi_ref, lam_ref):
    bw = xc.shape[1]
    gates = _dot(xc.astype(BF16), wg_ref[0])
    r = _sigmoid(gates[:, :bw] + bgr_ref[...])
    i = _sigmoid(gates[:, bw:] + bgi_ref[...])
    log_a = -LRU_C * r * _softplus(-lam_ref[...])
    a = jnp.exp(log_a)
    mult = jnp.sqrt(-jnp.tanh(log_a) * (a * a + 1.0))
    return a, mult * (i * xc)


def _scan8(a, b):
    row = lax.broadcasted_iota(jnp.int32, a.shape, 0)
    for s in (1, 2, 4):
        a_sh = pltpu.roll(a, s, 0)
        b_sh = pltpu.roll(b, s, 0)
        m = row >= s
        b = jnp.where(m, a * b_sh + b, b)
        a = jnp.where(m, a * a_sh, a)
    return a, b


def _lru_prompt_body(x_ref, y_ref, cw_ref, cb_ref, wg_ref, bgr_ref, bgi_ref, lam_ref,
                     o_ref, h_out_ref, carry_scr, h_scr, a_scr, b_scr):
    t = pl.program_id(2)

    @pl.when(t == 0)
    def _():
        carry_scr[...] = jnp.zeros_like(carry_scr)
        h_scr[...] = jnp.zeros_like(h_scr)

    x = x_ref[...]
    r = x.shape[0]
    xc = _causal_conv_rows(x, carry_scr[...], cw_ref[...]) + cb_ref[...]
    carry_scr[...] = x[r - SUBLANES:]
    a, b = _lru_coeffs(xc, wg_ref, bgr_ref, bgi_ref, lam_ref)
    a_scr[...] = a
    b_scr[...] = b

    def group(i, h_prev):
        rows = pl.ds(pl.multiple_of(i * SUBLANES, SUBLANES), SUBLANES)
        a8, b8 = _scan8(a_scr[rows, :], b_scr[rows, :])
        h8 = a8 * h_prev + b8
        b_scr[rows, :] = h8
        return jnp.broadcast_to(h8[SUBLANES - 1:SUBLANES, :], h8.shape)

    h_last = lax.fori_loop(0, r // SUBLANES, group, h_scr[...], unroll=8)
    h_scr[...] = h_last
    h_out_ref[0] = h_last[0:1, :]
    o_ref[...] = (b_scr[...] * _gelu_tanh(y_ref[...])).astype(o_ref.dtype)


def _lru_prompt(proj, conv_w, conv_b, w_gate, b_gate, lam, m_total, batch, seq, rows_per_step):
    nb, bw = w_gate.shape[0], w_gate.shape[1]
    drnn = nb * bw
    r = rows_per_step
    nt = seq // r
    row = lambda b, n, t: b * nt + t
    vec = lambda off: pl.BlockSpec((1, bw), lambda b, n, t: (0, off + n))
    return pl.pallas_call(
        _lru_prompt_body,
        grid=(batch, nb, nt),
        in_specs=[
            pl.BlockSpec((r, bw), lambda b, n, t: (row(b, n, t), n)),
            pl.BlockSpec((r, bw), lambda b, n, t: (row(b, n, t), nb + n)),
            pl.BlockSpec((conv_w.shape[0], bw), lambda b, n, t: (0, n)),
            vec(0),
            pl.BlockSpec((1, bw, 2 * bw), lambda b, n, t: (n, 0, 0)),
            vec(0), vec(nb), vec(0),
        ],
        out_specs=[
            pl.BlockSpec((r, bw), lambda b, n, t: (row(b, n, t), n)),
            pl.BlockSpec((1, 1, bw), lambda b, n, t: (b, 0, n)),
        ],
        out_shape=[
            jax.ShapeDtypeStruct((m_total, drnn), BF16),
            jax.ShapeDtypeStruct((batch, 1, drnn), F32),
        ],
        scratch_shapes=[pltpu.VMEM((SUBLANES, bw), F32), pltpu.VMEM((SUBLANES, bw), F32),
                        pltpu.VMEM((r, bw), F32), pltpu.VMEM((r, bw), F32)],
        compiler_params=_params("parallel", "parallel", "arbitrary"),
        name="lru_prompt",
    )(proj, proj, conv_w, conv_b.reshape(1, drnn), w_gate, b_gate.reshape(1, 2 * drnn),
      b_gate.reshape(1, 2 * drnn), lam.reshape(1, drnn))


def _lru_sample_body(x_ref, y_ref, st_ref, h0_ref, cw_ref, cb_ref, wg_ref, bgr_ref, bgi_ref, lam_ref,
                     o_full_ref, o_ref, h_out_ref):
    del o_full_ref
    xc = _conv_from_state(st_ref, x_ref[...], cw_ref[...]) + cb_ref[...]
    a, b = _lru_coeffs(xc, wg_ref, bgr_ref, bgi_ref, lam_ref)
    h = a * h0_ref[...] + b
    h_out_ref[...] = h
    o_ref[...] = (h * _gelu_tanh(y_ref[...])).astype(o_ref.dtype)


def _lru_sample(proj, conv_state, h0, conv_w, conv_b, w_gate, b_gate, lam, o_full, row0):
    nsamp = h0.shape[0]
    nb, bw = w_gate.shape[0], w_gate.shape[1]
    drnn = nb * bw
    kc = conv_w.shape[0]
    assert row0 % nsamp == 0
    rb0 = row0 // nsamp
    vec = lambda off: pl.BlockSpec((1, bw), lambda n: (0, off + n))
    return pl.pallas_call(
        _lru_sample_body,
        grid=(nb,),
        in_specs=[
            pl.BlockSpec((nsamp, bw), lambda n: (rb0, n)),
            pl.BlockSpec((nsamp, bw), lambda n: (rb0, nb + n)),
            pl.BlockSpec((kc - 1, nsamp, bw), lambda n: (0, 0, n)),
            pl.BlockSpec((nsamp, bw), lambda n: (0, n)),
            pl.BlockSpec((kc, bw), lambda n: (0, n)),
            vec(0),
            pl.BlockSpec((1, bw, 2 * bw), lambda n: (n, 0, 0)),
            vec(0), vec(nb), vec(0),
            pl.BlockSpec(memory_space=pl.ANY),
        ],
        out_specs=[pl.BlockSpec((nsamp, bw), lambda n: (rb0, n)), pl.BlockSpec((nsamp, bw), lambda n: (0, n))],
        out_shape=[jax.ShapeDtypeStruct(o_full.shape, o_full.dtype), jax.ShapeDtypeStruct((nsamp, drnn), F32)],
        input_output_aliases={10: 0},
        compiler_params=_params("parallel"),
        name="lru_sample",
    )(proj, proj, conv_state, h0, conv_w, conv_b.reshape(1, drnn), w_gate, b_gate.reshape(1, 2 * drnn),
      b_gate.reshape(1, 2 * drnn), lam.reshape(1, drnn), o_full)


def _ffn_prompt_body(x_ref, wg32_ref, wv32_ref, cw_ref, cb_ref, act_ref, tail_ref, carry_scr, wg_ref, wv_ref, *,
                     tiles_per_seq, n_sub):
    i = pl.program_id(1)

    @pl.when(i == 0)
    def _():
        wg_ref[...] = wg32_ref[...].astype(BF16)
        wv_ref[...] = wv32_ref[...].astype(BF16)

    @pl.when(i % tiles_per_seq == 0)
    def _():
        carry_scr[...] = jnp.zeros_like(carry_scr)

    sub = x_ref.shape[0] // n_sub
    cw = cw_ref[...]
    cb = cb_ref[...]
    prev8 = carry_scr[...]
    for s in range(n_sub):
        rs = slice(s * sub, (s + 1) * sub)
        xs = x_ref[rs, :]
        g = _dot(xs, wg_ref[...])
        v = _dot(xs, wv_ref[...])
        gate = _causal_conv_rows(g, prev8, cw) + cb
        act_ref[rs, :] = (_gelu_erf(gate) * v).astype(act_ref.dtype)
        prev8 = g[sub - SUBLANES:]
    carry_scr[...] = prev8
    tail_ref[...] = prev8


def _ffn_prompt(h, w_in, layer, conv_w, conv_b, m_total, batch, seq):
    kd = h.shape[1]
    dff = conv_w.shape[1]
    tm = _pick(seq, 1024, 16)
    tn = _pick(dff, 256, LANES)
    nj, ni = dff // tn, (batch * seq) // tm
    n_sub = 4 if tm % (4 * SUBLANES * 2) == 0 else 1
    body = functools.partial(_ffn_prompt_body, tiles_per_seq=seq // tm, n_sub=n_sub)
    return pl.pallas_call(
        body,
        grid=(nj, ni),
        in_specs=[
            pl.BlockSpec((tm, kd), lambda j, i: (i, 0)),
            pl.BlockSpec((None, kd, tn), lambda j, i: (layer, 0, j)),
            pl.BlockSpec((None, kd, tn), lambda j, i: (layer, 0, nj + j)),
            pl.BlockSpec((conv_w.shape[0], tn), lambda j, i: (0, j)),
            pl.BlockSpec((1, tn), lambda j, i: (0, j)),
        ],
        out_specs=[
            pl.BlockSpec((tm, tn), lambda j, i: (i, j)),
            pl.BlockSpec((SUBLANES, tn), lambda j, i: (i, j)),
        ],
        out_shape=[
            jax.ShapeDtypeStruct((m_total, dff), BF16),
            jax.ShapeDtypeStruct((ni * SUBLANES, dff), F32),
        ],
        scratch_shapes=[pltpu.VMEM((SUBLANES, tn), F32), pltpu.VMEM((kd, tn), BF16), pltpu.VMEM((kd, tn), BF16)],
        compiler_params=_params("parallel", "arbitrary"),
        name="ffn_prompt",
    )(h, w_in, w_in, conv_w, conv_b.reshape(1, dff))


def _ffn_sample_body(g_ref, v_ref, st_ref, cw_ref, cb_ref, o_full_ref, o_ref):
    del o_full_ref
    gate = _conv_from_state(st_ref, g_ref[...], cw_ref[...]) + cb_ref[...]
    o_ref[...] = (_gelu_erf(gate) * v_ref[...]).astype(o_ref.dtype)


def _ffn_sample(proj, conv_state, conv_w, conv_b, o_full, row0, cols_per_step):
    nsamp = conv_state.shape[1]
    dff = conv_w.shape[1]
    kc = conv_w.shape[0]
    tc = cols_per_step
    nc = dff // tc
    assert row0 % nsamp == 0
    rb0 = row0 // nsamp
    return pl.pallas_call(
        _ffn_sample_body,
        grid=(nc,),
        in_specs=[
            pl.BlockSpec((nsamp, tc), lambda c: (0, c)),
            pl.BlockSpec((nsamp, tc), lambda c: (0, nc + c)),
            pl.BlockSpec((kc - 1, nsamp, tc), lambda c: (0, 0, c)),
            pl.BlockSpec((kc, tc), lambda c: (0, c)),
            pl.BlockSpec((1, tc), lambda c: (0, c)),
            pl.BlockSpec(memory_space=pl.ANY),
        ],
        out_specs=pl.BlockSpec((nsamp, tc), lambda c: (rb0, c)),
        out_shape=jax.ShapeDtypeStruct(o_full.shape, o_full.dtype),
        input_output_aliases={5: 0},
        compiler_params=_params("parallel"),
        name="ffn_sample",
    )(proj, proj, conv_state, conv_w, conv_b.reshape(1, dff), o_full)


def _new_conv_state(old_state, pre_rows):
    return jnp.concatenate([old_state[:, 1:], pre_rows[:, None, :]], axis=1)


def _prompt_conv_tail(proj, batch, seq, keep, cols):
    return jnp.stack([lax.slice(proj, ((b + 1) * seq - keep, 0), ((b + 1) * seq, cols)) for b in range(batch)])


def _sample_rows(proj, mp, cols):
    return lax.slice(proj, (mp, 0), (proj.shape[0], cols))


def _gdn_gate_weights(w_in, n_heads, first_col):
    n_layers, d_model = w_in.shape[0], w_in.shape[1]
    tail = lax.slice(w_in, (0, 0, first_col), (n_layers, d_model, first_col + 2 * n_heads))
    zeros = jnp.zeros((n_layers, d_model, LANES - n_heads), w_in.dtype)
    return jnp.concatenate([tail[:, :, :n_heads], zeros, tail[:, :, n_heads:], zeros], axis=2).astype(BF16)


def _gdn_layer(x, h, mp, batch, seq, j, s_all, s_new_all, conv_state, w_in, w_ab_bf, conv_w, a_log, dt_bias,
               norm_w, w_out):
    n_heads = a_log.shape[0]
    m_total = x.shape[0]
    vd = w_out.shape[1]
    cd = conv_w.shape[1]
    nsamp = s_all.shape[1]
    assert vd == n_heads * LANES and cd == 3 * vd, "kernels assume DK == DV == 128"
    kc = conv_w.shape[0]

    pad = lambda p: jnp.zeros((1, LANES), F32).at[0, :n_heads].set(p.astype(F32))
    expand = (jnp.arange(LANES)[:, None] == (jnp.arange(vd) // LANES)[None, :]).astype(BF16)

    proj = _matmul(h, w_in, j, n_cols=cd + vd, tn=512)
    ab = _matmul(h, w_ab_bf, j)

    gcb, beb, gc_t = _gdn_gates(ab, pad(a_log), pad(dt_bias), expand, 0, mp, True)
    o, s_p = _gdn_prompt(proj, gcb, gc_t, beb, conv_w, norm_w, m_total, batch, seq, n_heads,
                         rows_per_step=min(seq, 256), heads_per_step=min(n_heads, 4))

    gb_s, beb_s = _gdn_gates(ab, pad(a_log), pad(dt_bias), expand, mp, nsamp, False)
    state_t = jnp.transpose(conv_state, (1, 0, 2))
    o, s_new_all = _gdn_sample(proj, state_t, gb_s, beb_s, conv_w, norm_w, s_all, j, o, s_new_all, mp,
                               n_heads, samples_per_step=min(nsamp, 64))

    x = _matmul(o, w_out, j, residual=x, tn=512)

    conv_p = _prompt_conv_tail(proj, batch, seq, kc - 1, cd)
    conv_s = _new_conv_state(conv_state, _sample_rows(proj, mp, cd))
    return x, s_p, s_new_all, conv_p, conv_s


def _lru_layer(x, h, mp, batch, seq, j, h0, conv_state, w_in, b_in, conv_w, conv_b, w_gate_bf, b_gate, lam,
               w_out):
    m_total = x.shape[0]
    drnn = conv_w.shape[1]
    kc = conv_w.shape[0]
    proj = _matmul(h, w_in, j, bias=b_in, tn=512)
    o, h_p = _lru_prompt(proj, conv_w, conv_b, w_gate_bf, b_gate, lam, m_total, batch, seq,
                         rows_per_step=min(seq, 512))
    state_t = jnp.transpose(conv_state, (1, 0, 2))
    o, h_s = _lru_sample(proj, state_t, h0, conv_w, conv_b, w_gate_bf, b_gate, lam, o, mp)
    x = _matmul(o, w_out, j, residual=x, tn=512)
    conv_p = _prompt_conv_tail(proj, batch, seq, kc - 1, drnn)
    conv_s = _new_conv_state(conv_state, _sample_rows(proj, mp, drnn))
    return x, h_p.reshape(batch, drnn), h_s, conv_p, conv_s


def _ffn_layer(x, h, mp, batch, seq, i, conv_state, w_in, conv_w, conv_b, w_out):
    m_total = x.shape[0]
    dff = conv_w.shape[1]
    kc = conv_w.shape[0]
    nsamp = m_total - mp
    act, tails = _ffn_prompt(h, w_in, i, conv_w, conv_b, m_total, batch, seq)
    proj_s = _matmul(h, w_in, i, tn=512, row0=mp, rows=nsamp)
    state_t = jnp.transpose(conv_state, (1, 0, 2))
    act = _ffn_sample(proj_s, state_t, conv_w, conv_b, act, mp, _pick(dff, 5504, LANES))
    x = _matmul(act, w_out, i, residual=x, tn=512, tk=_pick(dff, 5504, LANES))
    tails = tails.reshape(batch, -1, SUBLANES, dff)
    conv_p = tails[:, -1, SUBLANES - (kc - 1):, :]
    conv_s = _new_conv_state(conv_state, proj_s[:, :dff])
    return x, conv_p, conv_s


def kernel(x_prompt, x_sample, state_gdn_S, state_gdn_conv, state_lru_h, state_lru_conv, state_ffn_conv, norm_mixer, norm_ffn, norm_final, gdn_w_in, gdn_conv_w, gdn_A_log, gdn_dt_bias, gdn_norm_w, gdn_w_out, lru_w_in, lru_b_in, lru_conv_w, lru_conv_b, lru_w_gate, lru_b_gate, lru_lambda, lru_w_out, ffn_w_in, ffn_conv_w, ffn_conv_b, ffn_w_out):
    batch, seq, d_model = x_prompt.shape
    nsamp = x_sample.shape[0]
    depth = norm_mixer.shape[0]
    mp = batch * seq
    x = jnp.concatenate([x_prompt.reshape(mp, d_model), x_sample.reshape(nsamp, d_model)], axis=0)

    n_heads = gdn_A_log.shape[1]
    gdn_w_ab_bf = _gdn_gate_weights(gdn_w_in, n_heads, gdn_w_in.shape[2] - 2 * n_heads)
    lru_w_gate_bf = lru_w_gate.astype(BF16)
    ffn_w_out = ffn_w_out.astype(BF16)

    gs_p, gc_p, gc_s = [], [], []
    gs_s = None
    lh_p, lh_s, lc_p, lc_s = [], [], [], []
    fc_p, fc_s = [], []
    for i in range(depth):
        j = i // 2
        h = _rmsnorm(x, norm_mixer[i], BF16)
        if i % 2 == 0:
            x, sp, gs_s, cp, cs = _gdn_layer(
                x, h, mp, batch, seq, j, state_gdn_S, gs_s, state_gdn_conv[j], gdn_w_in, gdn_w_ab_bf,
                gdn_conv_w[j], gdn_A_log[j], gdn_dt_bias[j], gdn_norm_w[j], gdn_w_out)
            gs_p.append(sp); gc_p.append(cp); gc_s.append(cs)
        else:
            x, hp, hs, cp, cs = _lru_layer(
                x, h, mp, batch, seq, j, state_lru_h[j], state_lru_conv[j], lru_w_in, lru_b_in[j],
                lru_conv_w[j], lru_conv_b[j], lru_w_gate_bf[j], lru_b_gate[j], lru_lambda[j], lru_w_out)
            lh_p.append(hp); lh_s.append(hs); lc_p.append(cp); lc_s.append(cs)
        h = _rmsnorm(x, norm_ffn[i], BF16)
        x, cp, cs = _ffn_layer(x, h, mp, batch, seq, i, state_ffn_conv[i], ffn_w_in, ffn_conv_w[i],
                               ffn_conv_b[i], ffn_w_out)
        fc_p.append(cp); fc_s.append(cs)

    y_prompt = _rmsnorm(x, norm_final, F32, 0, mp).reshape(batch, seq, d_model)
    y_sample = _rmsnorm(x, norm_final, F32, mp, nsamp).reshape(nsamp, 1, d_model)
    return (y_prompt, y_sample,
            jnp.stack(gs_p), gs_s, jnp.stack(gc_p), jnp.stack(gc_s),
            jnp.stack(lh_p), jnp.stack(lh_s), jnp.stack(lc_p), jnp.stack(lc_s),
            jnp.stack(fc_p), jnp.stack(fc_s))
```

```python
import functools
import math

import jax
import jax.numpy as jnp
from jax import lax
from jax.experimental import pallas as pl
from jax.experimental.pallas import tpu as pltpu

F32 = jnp.float32
BF16 = jnp.bfloat16
HIGHEST = lax.Precision.HIGHEST

NORM_EPS = 1e-6
L2_EPS = 1e-6
LRU_C = 8.0
GDN_CHUNK = 64

LANES = 128
SUBLANES = 8
VMEM_LIMIT_BYTES = 56 * 1024 * 1024


def _params(*sem):
    return pltpu.CompilerParams(dimension_semantics=sem, vmem_limit_bytes=VMEM_LIMIT_BYTES)


def _pick(n, target, align):
    best = None
    t = align
    while t <= min(n, target):
        if n % t == 0:
            best = t
        t += align
    return best if best is not None else n


def _dot(a, b, precision=None):
    return jnp.dot(a, b, preferred_element_type=F32, precision=precision)


def _dot_nt(a, b, precision=None):
    return lax.dot_general(a, b, (((1,), (1,)), ((), ())), preferred_element_type=F32, precision=precision)


def _softplus(x):
    return jnp.maximum(x, 0.0) + jnp.log1p(jnp.exp(-jnp.abs(x)))


def _sigmoid(x):
    return 1.0 / (1.0 + jnp.exp(-x))


def _silu(x):
    return x * _sigmoid(x)


def _gelu_tanh(x):
    c = math.sqrt(2.0 / math.pi)
    return 0.5 * x * (1.0 + jnp.tanh(c * (x + 0.044715 * (x * x * x))))


def _gelu_erf(x):
    return 0.5 * x * (1.0 + lax.erf(x * (1.0 / math.sqrt(2.0))))


def _rmsnorm_body(x_ref, w_ref, o_ref):
    x = x_ref[...]
    y = x * lax.rsqrt(jnp.mean(x * x, axis=-1, keepdims=True) + NORM_EPS)
    o_ref[...] = (y * w_ref[...]).astype(o_ref.dtype)


def _rmsnorm(x, w, out_dtype, row0=0, rows=None):
    d = x.shape[1]
    rows = x.shape[0] if rows is None else rows
    tr = _pick(math.gcd(rows, row0) if row0 else rows, 512, 16)
    rb0 = row0 // tr
    return pl.pallas_call(
        _rmsnorm_body,
        grid=(rows // tr,),
        in_specs=[pl.BlockSpec((tr, d), lambda i: (rb0 + i, 0)), pl.BlockSpec((1, d), lambda i: (0, 0))],
        out_specs=pl.BlockSpec((tr, d), lambda i: (i, 0)),
        out_shape=jax.ShapeDtypeStruct((rows, d), out_dtype),
        compiler_params=_params("parallel"),
        name="rmsnorm",
    )(x, w.reshape(1, d))


def _matmul_body(*refs, nk, has_bias, has_res, cast_w, w_rows_are_outputs):
    x_ref, w_ref = refs[0], refs[1]
    pos = 2
    b_ref = r_ref = None
    if has_bias:
        b_ref = refs[pos]
        pos += 1
    if has_res:
        r_ref = refs[pos]
        pos += 1
    o_ref = refs[pos]

    def finish(acc):
        if has_bias:
            acc = acc + b_ref[...]
        if has_res:
            acc = r_ref[...] + acc
        o_ref[...] = acc.astype(o_ref.dtype)

    if cast_w:
        wb_ref = refs[pos + 1]

        @pl.when(pl.program_id(1) == 0)
        def _():
            wb_ref[...] = w_ref[...].astype(BF16)

        finish((_dot_nt if w_rows_are_outputs else _dot)(x_ref[...], wb_ref[...]))
        return

    part = _dot(x_ref[...], w_ref[...])
    if nk == 1:
        finish(part)
    else:
        acc_ref = refs[pos + 1]
        k = pl.program_id(2)

        @pl.when(k == 0)
        def _():
            acc_ref[...] = part

        @pl.when(k > 0)
        def _():
            acc_ref[...] += part

        @pl.when(k == nk - 1)
        def _():
            finish(acc_ref[...])


def _matmul(x, w, layer, bias=None, residual=None, n_cols=None, out_dtype=F32, tm=832, tn=1024, tk=4096,
            row0=0, rows=None, w_rows_are_outputs=False):
    kd = x.shape[1]
    m = x.shape[0] if rows is None else rows
    n = (w.shape[1] if w_rows_are_outputs else w.shape[2]) if n_cols is None else n_cols
    cast_w = w.dtype != BF16
    assert cast_w or not w_rows_are_outputs
    tm = _pick(math.gcd(m, row0) if row0 else m, tm, 16)
    tn = _pick(n, tn, LANES)
    tk = kd if cast_w else _pick(kd, tk, LANES)
    nk = kd // tk
    rb0 = row0 // tm
    assert residual is None or row0 == 0
    if w_rows_are_outputs:
        w_spec = pl.BlockSpec((None, tn, tk), lambda j, i, k: (layer, j, k))
    else:
        w_spec = pl.BlockSpec((None, tk, tn), lambda j, i, k: (layer, k, j))
    in_specs = [pl.BlockSpec((tm, tk), lambda j, i, k: (rb0 + i, k)), w_spec]
    args = [x, w]
    if bias is not None:
        in_specs.append(pl.BlockSpec((1, tn), lambda j, i, k: (0, j)))
        args.append(bias.reshape(1, n).astype(F32))
    if residual is not None:
        in_specs.append(pl.BlockSpec((tm, tn), lambda j, i, k: (i, j)))
        args.append(residual)
    body = functools.partial(_matmul_body, nk=nk, has_bias=bias is not None, has_res=residual is not None,
                             cast_w=cast_w, w_rows_are_outputs=w_rows_are_outputs)
    if cast_w:
        scratch = [pltpu.VMEM((tn, tk) if w_rows_are_outputs else (tk, tn), BF16)]
    else:
        scratch = [pltpu.VMEM((tm, tn), F32)] if nk > 1 else []
    return pl.pallas_call(
        body,
        grid=(n // tn, m // tm, nk),
        in_specs=in_specs,
        out_specs=pl.BlockSpec((tm, tn), lambda j, i, k: (i, j)),
        out_shape=jax.ShapeDtypeStruct((m, n), out_dtype),
        scratch_shapes=scratch,
        compiler_params=_params("parallel", "arbitrary", "arbitrary"),
        name="matmul",
    )(*args)


def _shift_rows(x, prev8, s):
    if s == 0:
        return x
    xs = pltpu.roll(x, s, 0)
    row = lax.broadcasted_iota(jnp.int32, prev8.shape, 0)
    top = jnp.where(row < s, pltpu.roll(prev8, s, 0), xs[:SUBLANES])
    if x.shape[0] == SUBLANES:
        return top
    return jnp.concatenate([top, xs[SUBLANES:]], axis=0)


def _causal_conv_rows(x, prev8, w):
    kc = w.shape[0]
    y = None
    for kk in range(kc):
        term = _shift_rows(x, prev8, kc - 1 - kk) * w[kk:kk + 1]
        y = term if y is None else y + term
    return y


def _conv_from_state(state_ref, x, w):
    kc = w.shape[0]
    y = state_ref[0] * w[0:1]
    for kk in range(1, kc - 1):
        y = y + state_ref[kk] * w[kk:kk + 1]
    return y + x * w[kc - 1:kc]


def _expand_lanes(x, e):
    hi = x.astype(BF16)
    rest = x - hi.astype(F32)
    mid = rest.astype(BF16)
    lo = (rest - mid.astype(F32)).astype(BF16)
    return _dot(hi, e) + _dot(mid, e) + _dot(lo, e)


def _gdn_gate_body(a_ref, b_ref, alog_ref, dtb_ref, e_ref, *out_refs, chunk, cumulative):
    g = -jnp.exp(alog_ref[...]) * _softplus(a_ref[...] + dtb_ref[...])
    beta = _sigmoid(b_ref[...])
    e = e_ref[...]
    if cumulative:
        r = g.shape[0]
        ri = lax.broadcasted_iota(jnp.int32, (chunk, chunk), 0)
        ci = lax.broadcasted_iota(jnp.int32, (chunk, chunk), 1)
        tril = (ri >= ci).astype(F32)
        gc = jnp.concatenate(
            [_dot(tril, g[c * chunk:(c + 1) * chunk], HIGHEST) for c in range(r // chunk)], axis=0)
        out_refs[0][...] = _expand_lanes(gc, e)
        out_refs[1][...] = _expand_lanes(beta, e)
        out_refs[2][...] = jnp.transpose(gc)
    else:
        out_refs[0][...] = _expand_lanes(g, e)
        out_refs[1][...] = _expand_lanes(beta, e)


def _gdn_gates(ab, alog_pad, dtb_pad, expand, row0, rows, cumulative):
    hd = expand.shape[1]
    tr = _pick(rows, 256, LANES)
    cw = _pick(hd, 1024, LANES)
    rb0 = row0 // tr
    assert row0 % tr == 0 and tr % GDN_CHUNK == 0
    wide = pl.BlockSpec((tr, cw), lambda i, j: (i, j))
    out_specs = [wide, wide]
    out_shape = [jax.ShapeDtypeStruct((rows, hd), F32)] * 2
    if cumulative:
        out_specs.append(pl.BlockSpec((LANES, tr), lambda i, j: (0, i)))
        out_shape.append(jax.ShapeDtypeStruct((LANES, rows), F32))
    body = functools.partial(_gdn_gate_body, chunk=GDN_CHUNK, cumulative=cumulative)
    return pl.pallas_call(
        body,
        grid=(rows // tr, hd // cw),
        in_specs=[
            pl.BlockSpec((tr, LANES), lambda i, j: (rb0 + i, 0)),
            pl.BlockSpec((tr, LANES), lambda i, j: (rb0 + i, 1)),
            pl.BlockSpec((1, LANES), lambda i, j: (0, 0)),
            pl.BlockSpec((1, LANES), lambda i, j: (0, 0)),
            pl.BlockSpec((LANES, cw), lambda i, j: (0, j)),
        ],
        out_specs=out_specs,
        out_shape=out_shape,
        compiler_params=_params("parallel", "arbitrary"),
        name="gdn_gates",
    )(ab, ab, alog_pad, dtb_pad, expand)


def _gated_head_norm(o, z, nw):
    o = o * lax.rsqrt(jnp.mean(o * o, axis=-1, keepdims=True) + NORM_EPS) * nw
    return o * _silu(z)


def _gdn_block_masks(r, chunk, dk):
    ri = lax.broadcasted_iota(jnp.int32, (r, r), 0)
    ci = lax.broadcasted_iota(jnp.int32, (r, r), 1)
    first = (ri // chunk) * chunk
    causal = (ci >= first) & (ci <= ri)
    strict = (ci >= first) & (ci < ri)
    eye = (ri == ci).astype(F32)
    bi = lax.broadcasted_iota(jnp.int32, (r // chunk * dk, r), 0) // dk
    bj = lax.broadcasted_iota(jnp.int32, (r // chunk * dk, r), 1) // chunk
    return causal, strict, eye, bi == bj


def _gdn_block_terms(qs, ks, vs, gcbs, gc_rows, bebs, chunk, masks):
    nh = len(qs)
    hs = range(nh)
    r = ks[0].shape[0]
    dv = vs[0].shape[1]
    nch = r // chunk
    causal, strict, eye, kt_mask = masks
    decay = [jnp.where(causal, jnp.exp(jnp.where(
        causal, jnp.concatenate([gcbs[h]] * (r // LANES), axis=1) - gc_rows[h], 0.0)), 0.0) for h in hs]
    eg = [jnp.exp(gcbs[h]) for h in hs]
    kb = [ks[h] * bebs[h] for h in hs]
    kq = [_dot_nt(jnp.concatenate([kb[h], qs[h]], axis=0).astype(BF16), ks[h].astype(BF16)) for h in hs]
    qk = [kq[h][r:] * decay[h] for h in hs]
    n = [-jnp.where(strict, kq[h][:r] * decay[h], 0.0) for h in hs]
    tm = [eye + n[h] for h in hs]
    nb = [n[h].astype(BF16) for h in hs]
    p = [_dot(nb[h], nb[h]) for h in hs]
    steps = int(math.log2(chunk)) - 1
    for i in range(steps):
        pb = [p[h].astype(BF16) for h in hs]
        if i < steps - 1:
            both = [_dot(jnp.concatenate([pb[h], tm[h].astype(BF16)], axis=0), pb[h]) for h in hs]
            p = [both[h][:r] for h in hs]
            tm = [tm[h] + both[h][r:] for h in hs]
        else:
            tm = [tm[h] + _dot(tm[h].astype(BF16), pb[h]) for h in hs]
    uwb = [_dot(tm[h].astype(BF16), jnp.concatenate([vs[h] * bebs[h], kb[h] * eg[h]], axis=1).astype(BF16)
                ).astype(BF16) for h in hs]
    gl = [jnp.concatenate(
        [jnp.broadcast_to(gcbs[h][(c + 1) * chunk - 1:(c + 1) * chunk, :], (chunk, LANES)) for c in range(nch)],
        axis=0) for h in hs]
    k_tail_t = [jnp.transpose(ks[h] * jnp.exp(gl[h] - gcbs[h])) for h in hs]
    kt_blocks = [jnp.where(kt_mask, jnp.concatenate([k_tail_t[h]] * nch, axis=0), 0.0) for h in hs]
    both = [_dot(jnp.concatenate([qk[h], kt_blocks[h]], axis=0).astype(BF16), uwb[h]) for h in hs]
    return [(qs[h] * eg[h] - both[h][:r, dv:], both[h][:r, :dv], both[h][r:, dv:], both[h][r:, :dv], gl[h])
            for h in hs]


def _gdn_prompt_body(q_ref, k_ref, v_ref, z_ref, gc_ref, gr_ref, be_ref, wq_ref, wk_ref, wv_ref, nw_ref,
                     o_init_ref, o_ref, s_out_ref, s_scr, carry_scr, *, heads, chunk, q_scale):
    del o_init_ref
    t = pl.program_id(2)

    @pl.when(t == 0)
    def _():
        s_scr[...] = jnp.zeros_like(s_scr)
        carry_scr[...] = jnp.zeros_like(carry_scr)

    r = q_ref.shape[0]

    def conv_silu(x_ref, w_ref, idx):
        x = x_ref[...]
        y = _causal_conv_rows(x, carry_scr[idx], w_ref[...])
        carry_scr[idx] = x[r - SUBLANES:]
        return _silu(y)

    qc = conv_silu(q_ref, wq_ref, 0)
    kc = conv_silu(k_ref, wk_ref, 1)
    vc = conv_silu(v_ref, wv_ref, 2)

    nw = nw_ref[...]
    dk = LANES
    masks = _gdn_block_masks(r, chunk, dk)

    lanes = [slice(h * LANES, (h + 1) * LANES) for h in range(heads)]
    qs, ks = [], []
    for ls in lanes:
        qh = qc[:, ls]
        kh = kc[:, ls]
        qs.append(qh * lax.rsqrt(jnp.sum(qh * qh, axis=-1, keepdims=True) + L2_EPS) * q_scale)
        ks.append(kh * lax.rsqrt(jnp.sum(kh * kh, axis=-1, keepdims=True) + L2_EPS))
    terms = _gdn_block_terms(qs, ks, [vc[:, ls] for ls in lanes], [gc_ref[:, ls] for ls in lanes],
                             [gr_ref[h, 0] for h in range(heads)], [be_ref[:, ls] for ls in lanes], chunk, masks)
    s = [s_scr[h] for h in range(heads)]
    for c in range(r // chunk):
        rs = slice(c * chunk, (c + 1) * chunk)
        bs = slice(c * dk, (c + 1) * dk)
        for h, ls in enumerate(lanes):
            q_eff, o0, m_mat, b_mat, gl = terms[h]
            lhs = jnp.concatenate([q_eff[rs], m_mat[bs]], axis=0).astype(BF16)
            ls_s = _dot(lhs, s[h].astype(BF16))
            o = ls_s[:chunk] + o0[rs]
            s_decay = jnp.broadcast_to(jnp.exp(gl[c * chunk:c * chunk + 1, :]), s[h].shape)
            s[h] = s[h] * s_decay - ls_s[chunk:] + b_mat[bs]
            o_ref[rs, ls] = _gated_head_norm(o, z_ref[rs, ls], nw).astype(o_ref.dtype)
    for h in range(heads):
        s_scr[h] = s[h]
        s_out_ref[0, h] = s[h]


def _gdn_prompt(proj, gcb, gc_t, beb, conv_w, norm_w, m_total, batch, seq, n_heads, rows_per_step,
                heads_per_step):
    hd = n_heads * LANES
    g = heads_per_step
    gw = g * LANES
    r = rows_per_step
    nt = seq // r
    ncb = hd // gw
    assert r % LANES == 0 and r % GDN_CHUNK == 0
    gc_rows = gc_t.reshape(LANES, (batch * seq) // r, 1, r)
    row = lambda b, hg, t: b * nt + t
    sect = lambda s: pl.BlockSpec((r, gw), lambda b, hg, t: (row(b, hg, t), s * ncb + hg))
    gate = pl.BlockSpec((r, gw), lambda b, hg, t: (row(b, hg, t), hg))
    wsp = lambda s: pl.BlockSpec((conv_w.shape[0], gw), lambda b, hg, t: (0, s * ncb + hg))
    body = functools.partial(_gdn_prompt_body, heads=g, chunk=GDN_CHUNK, q_scale=float(LANES) ** -0.5)
    return pl.pallas_call(
        body,
        grid=(batch, n_heads // g, nt),
        in_specs=[sect(0), sect(1), sect(2), sect(3), gate,
                  pl.BlockSpec((g, 1, 1, r), lambda b, hg, t: (hg, row(b, hg, t), 0, 0)),
                  gate, wsp(0), wsp(1), wsp(2),
                  pl.BlockSpec((1, LANES), lambda b, hg, t: (0, 0)),
                  pl.BlockSpec(memory_space=pl.ANY)],
        out_specs=[
            pl.BlockSpec((r, gw), lambda b, hg, t: (row(b, hg, t), hg)),
            pl.BlockSpec((1, g, LANES, LANES), lambda b, hg, t: (b, hg, 0, 0)),
        ],
        out_shape=[
            jax.ShapeDtypeStruct((m_total, hd), BF16),
            jax.ShapeDtypeStruct((batch, n_heads, LANES, LANES), F32),
        ],
        input_output_aliases={11: 0},
        scratch_shapes=[pltpu.VMEM((g, LANES, LANES), F32), pltpu.VMEM((3, SUBLANES, gw), F32)],
        compiler_params=_params("parallel", "parallel", "arbitrary"),
        name="gdn_prompt",
    )(proj, proj, proj, proj, gcb, gc_rows, beb, conv_w, conv_w, conv_w, norm_w.reshape(1, LANES),
      jnp.zeros((m_total, hd), BF16))


def _gdn_sample_body(q_ref, k_ref, v_ref, z_ref, sq_ref, sk_ref, sv_ref, g_ref, be_ref, wq_ref, wk_ref, wv_ref,
                     nw_ref, s_ref, *rest, q_scale):
    o_ref, s_out_ref, q_scr, k_scr, d_scr, o_scr = rest[-6:]
    nb = q_ref.shape[0]
    q = _silu(_conv_from_state(sq_ref, q_ref[...], wq_ref[...]))
    k = _silu(_conv_from_state(sk_ref, k_ref[...], wk_ref[...]))
    v = _silu(_conv_from_state(sv_ref, v_ref[...], wv_ref[...]))
    q_scr[...] = q * lax.rsqrt(jnp.sum(q * q, axis=-1, keepdims=True) + L2_EPS) * q_scale
    k_scr[...] = k * lax.rsqrt(jnp.sum(k * k, axis=-1, keepdims=True) + L2_EPS)
    d_scr[0] = jnp.exp(g_ref[...])
    d_scr[1] = be_ref[...]
    d_scr[2] = v

    def step(b, carry):
        row = pl.ds(b, 1)
        k_col = jnp.transpose(jnp.broadcast_to(k_scr[row, :], (LANES, LANES)))
        q_col = jnp.transpose(jnp.broadcast_to(q_scr[row, :], (LANES, LANES)))
        s = s_ref[b] * d_scr[0, row, :]
        ks = jnp.sum(k_col * s, axis=0, keepdims=True)
        delta = d_scr[1, row, :] * (d_scr[2, row, :] - ks)
        s = s + k_col * delta
        s_out_ref[b] = s
        o_scr[row, :] = jnp.sum(q_col * s, axis=0, keepdims=True)
        return carry

    lax.fori_loop(0, nb, step, 0, unroll=4)
    o_ref[...] = _gated_head_norm(o_scr[...], z_ref[...], nw_ref[...]).astype(o_ref.dtype)


def _gdn_sample(proj, conv_state, gb, beb, conv_w, norm_w, s_all, layer, o_full, s_new_all, row0, n_heads,
                samples_per_step):
    nsamp = s_all.shape[1]
    bs = samples_per_step
    assert row0 % bs == 0 and nsamp % bs == 0
    rb0 = row0 // bs
    nh = n_heads
    kc = conv_w.shape[0]
    sect = lambda s: pl.BlockSpec((bs, LANES), lambda h, i: (rb0 + i, s * nh + h))
    stat = lambda s: pl.BlockSpec((kc - 1, bs, LANES), lambda h, i: (0, i, s * nh + h))
    gate = pl.BlockSpec((bs, LANES), lambda h, i: (i, h))
    wsp = lambda s: pl.BlockSpec((kc, LANES), lambda h, i: (0, s * nh + h))
    sspec = pl.BlockSpec((None, bs, None, LANES, LANES), lambda h, i: (layer, i, h, 0, 0))
    untouched = pl.BlockSpec(memory_space=pl.ANY)
    in_specs = [sect(0), sect(1), sect(2), sect(3), stat(0), stat(1), stat(2), gate, gate,
                wsp(0), wsp(1), wsp(2), pl.BlockSpec((1, LANES), lambda h, i: (0, 0)), sspec, untouched]
    args = [proj, proj, proj, proj, conv_state, conv_state, conv_state, gb, beb, conv_w, conv_w, conv_w,
            norm_w.reshape(1, LANES), s_all, o_full]
    aliases = {len(args) - 1: 0}
    if s_new_all is not None:
        in_specs.append(untouched)
        args.append(s_new_all)
        aliases[len(args) - 1] = 1
    body = functools.partial(_gdn_sample_body, q_scale=float(LANES) ** -0.5)
    return pl.pallas_call(
        body,
        grid=(nh, nsamp // bs),
        in_specs=in_specs,
        out_specs=[pl.BlockSpec((bs, LANES), lambda h, i: (rb0 + i, h)), sspec],
        out_shape=[
            jax.ShapeDtypeStruct(o_full.shape, o_full.dtype),
            jax.ShapeDtypeStruct(s_all.shape, F32),
        ],
        input_output_aliases=aliases,
        scratch_shapes=[pltpu.VMEM((bs, LANES), F32), pltpu.VMEM((bs, LANES), F32),
                        pltpu.VMEM((3, bs, LANES), F32), pltpu.VMEM((bs, LANES), F32)],
        compiler_params=_params("parallel", "parallel"),
        name="gdn_sample",
    )(*args)


def _lru_coeffs(xc, wg_ref, bgr_ref, bgi_ref, lam_ref):
    bw = xc.shape[1]
    gates = _dot(xc.astype(BF16), wg_ref[0])
    r = _sigmoid(gates[:, :bw] + bgr_ref[...])
    i = _sigmoid(gates[:, bw:] + bgi_ref[...])
    log_a = -LRU_C * r * _softplus(-lam_ref[...])
    a = jnp.exp(log_a)
    mult = jnp.sqrt(-jnp.tanh(log_a) * (a * a + 1.0))
    return a, mult * (i * xc)


def _scan8(a, b):
    row = lax.broadcasted_iota(jnp.int32, a.shape, 0)
    for s in (1, 2, 4):
        a_sh = pltpu.roll(a, s, 0)
        b_sh = pltpu.roll(b, s, 0)
        m = row >= s
        b = jnp.where(m, a * b_sh + b, b)
        a = jnp.where(m, a * a_sh, a)
    return a, b


def _lru_prompt_body(x_ref, y_ref, cw_ref, cb_ref, wg_ref, bgr_ref, bgi_ref, lam_ref, o_init_ref,
                     o_ref, h_out_ref, carry_scr, h_scr, a_scr, b_scr):
    del o_init_ref
    t = pl.program_id(2)

    @pl.when(t == 0)
    def _():
        carry_scr[...] = jnp.zeros_like(carry_scr)
        h_scr[...] = jnp.zeros_like(h_scr)

    x = x_ref[...]
    r = x.shape[0]
    xc = _causal_conv_rows(x, carry_scr[...], cw_ref[...]) + cb_ref[...]
    carry_scr[...] = x[r - SUBLANES:]
    a, b = _lru_coeffs(xc, wg_ref, bgr_ref, bgi_ref, lam_ref)
    a_scr[...] = a
    b_scr[...] = b

    def group(i, h_prev):
        rows = pl.ds(pl.multiple_of(i * SUBLANES, SUBLANES), SUBLANES)
        a8, b8 = _scan8(a_scr[rows, :], b_scr[rows, :])
        h8 = a8 * h_prev + b8
        b_scr[rows, :] = h8
        return jnp.broadcast_to(h8[SUBLANES - 1:SUBLANES, :], h8.shape)

    h_last = lax.fori_loop(0, r // SUBLANES, group, h_scr[...], unroll=8)
    h_scr[...] = h_last
    h_out_ref[0] = h_last[0:1, :]
    o_ref[...] = (b_scr[...] * _gelu_tanh(y_ref[...])).astype(o_ref.dtype)


def _lru_prompt(proj, conv_w, conv_b, w_gate, b_gate, lam, m_total, batch, seq, rows_per_step):
    nb, bw = w_gate.shape[0], w_gate.shape[1]
    drnn = nb * bw
    r = rows_per_step
    nt = seq // r
    row = lambda b, n, t: b * nt + t
    vec = lambda off: pl.BlockSpec((1, bw), lambda b, n, t: (0, off + n))
    return pl.pallas_call(
        _lru_prompt_body,
        grid=(batch, nb, nt),
        in_specs=[
            pl.BlockSpec((r, bw), lambda b, n, t: (row(b, n, t), n)),
            pl.BlockSpec((r, bw), lambda b, n, t: (row(b, n, t), nb + n)),
            pl.BlockSpec((conv_w.shape[0], bw), lambda b, n, t: (0, n)),
            vec(0),
            pl.BlockSpec((1, bw, 2 * bw), lambda b, n, t: (n, 0, 0)),
            vec(0), vec(nb), vec(0),
            pl.BlockSpec(memory_space=pl.ANY),
        ],
        out_specs=[
            pl.BlockSpec((r, bw), lambda b, n, t: (row(b, n, t), n)),
            pl.BlockSpec((1, 1, bw), lambda b, n, t: (b, 0, n)),
        ],
        out_shape=[
            jax.ShapeDtypeStruct((m_total, drnn), BF16),
            jax.ShapeDtypeStruct((batch, 1, drnn), F32),
        ],
        input_output_aliases={8: 0},
        scratch_shapes=[pltpu.VMEM((SUBLANES, bw), F32), pltpu.VMEM((SUBLANES, bw), F32),
                        pltpu.VMEM((r, bw), F32), pltpu.VMEM((r, bw), F32)],
        compiler_params=_params("parallel", "parallel", "arbitrary"),
        name="lru_prompt",
    )(proj, proj, conv_w, conv_b.reshape(1, drnn), w_gate, b_gate.reshape(1, 2 * drnn),
      b_gate.reshape(1, 2 * drnn), lam.reshape(1, drnn), jnp.zeros((m_total, drnn), BF16))


def _lru_sample_body(x_ref, y_ref, st_ref, h0_ref, cw_ref, cb_ref, wg_ref, bgr_ref, bgi_ref, lam_ref,
                     o_full_ref, o_ref, h_out_ref):
    del o_full_ref
    xc = _conv_from_state(st_ref, x_ref[...], cw_ref[...]) + cb_ref[...]
    a, b = _lru_coeffs(xc, wg_ref, bgr_ref, bgi_ref, lam_ref)
    h = a * h0_ref[...] + b
    h_out_ref[...] = h
    o_ref[...] = (h * _gelu_tanh(y_ref[...])).astype(o_ref.dtype)


def _lru_sample(proj, conv_state, h0, conv_w, conv_b, w_gate, b_gate, lam, o_full, row0):
    nsamp = h0.shape[0]
    nb, bw = w_gate.shape[0], w_gate.shape[1]
    drnn = nb * bw
    kc = conv_w.shape[0]
    assert row0 % nsamp == 0
    rb0 = row0 // nsamp
    vec = lambda off: pl.BlockSpec((1, bw), lambda n: (0, off + n))
    return pl.pallas_call(
        _lru_sample_body,
        grid=(nb,),
        in_specs=[
            pl.BlockSpec((nsamp, bw), lambda n: (rb0, n)),
            pl.BlockSpec((nsamp, bw), lambda n: (rb0, nb + n)),
            pl.BlockSpec((kc - 1, nsamp, bw), lambda n: (0, 0, n)),
            pl.BlockSpec((nsamp, bw), lambda n: (0, n)),
            pl.BlockSpec((kc, bw), lambda n: (0, n)),
            vec(0),
            pl.BlockSpec((1, bw, 2 * bw), lambda n: (n, 0, 0)),
            vec(0), vec(nb), vec(0),
            pl.BlockSpec(memory_space=pl.ANY),
        ],
        out_specs=[pl.BlockSpec((nsamp, bw), lambda n: (rb0, n)), pl.BlockSpec((nsamp, bw), lambda n: (0, n))],
        out_shape=[jax.ShapeDtypeStruct(o_full.shape, o_full.dtype), jax.ShapeDtypeStruct((nsamp, drnn), F32)],
        input_output_aliases={10: 0},
        compiler_params=_params("parallel"),
        name="lru_sample",
    )(proj, proj, conv_state, h0, conv_w, conv_b.reshape(1, drnn), w_gate, b_gate.reshape(1, 2 * drnn),
      b_gate.reshape(1, 2 * drnn), lam.reshape(1, drnn), o_full)


def _ffn_body(x_ref, wg32_ref, wv32_ref, cw_ref, cb_ref, st_ref, act_ref, tail_ref, gs_ref, carry_scr, wg_ref,
              wv_ref, *, tiles_per_seq, n_prompt_tiles, n_sub):
    i = pl.program_id(1)
    nsamp = gs_ref.shape[0]

    @pl.when(i == 0)
    def _():
        wg_ref[...] = wg32_ref[...].astype(BF16)
        wv_ref[...] = wv32_ref[...].astype(BF16)

    @pl.when(i % tiles_per_seq == 0)
    def _():
        carry_scr[...] = jnp.zeros_like(carry_scr)

    cw = cw_ref[...]
    cb = cb_ref[...]

    @pl.when(i < n_prompt_tiles)
    def _():
        sub = x_ref.shape[0] // n_sub
        rows = [slice(s * sub, (s + 1) * sub) for s in range(n_sub)]

        def project(s):
            xs = x_ref[rows[s], :]
            return _dot(xs, wg_ref[...]), _dot(xs, wv_ref[...])

        prev8 = carry_scr[...]
        g, v = project(0)
        for s in range(n_sub):
            nxt = project(s + 1) if s + 1 < n_sub else None
            gate = _causal_conv_rows(g, prev8, cw) + cb
            act_ref[rows[s], :] = (_gelu_erf(gate) * v).astype(act_ref.dtype)
            prev8 = g[sub - SUBLANES:]
            if nxt is not None:
                g, v = nxt
        carry_scr[...] = prev8
        tail_ref[...] = prev8

    @pl.when(i == n_prompt_tiles)
    def _():
        xs = x_ref[:nsamp, :]
        g = _dot(xs, wg_ref[...])
        v = _dot(xs, wv_ref[...])
        gate = _conv_from_state(st_ref, g, cw) + cb
        act_ref[:nsamp, :] = (_gelu_erf(gate) * v).astype(act_ref.dtype)
        gs_ref[...] = g
        tail_ref[...] = jnp.zeros_like(tail_ref)


def _ffn_in(h, w_in, layer, conv_w, conv_b, conv_state, batch, seq):
    m_total, kd = h.shape
    dff = conv_w.shape[1]
    kc = conv_w.shape[0]
    mp = batch * seq
    nsamp = m_total - mp
    tm = _pick(seq, 1024, 16)
    tn = _pick(dff, 256, LANES)
    nj, ni = dff // tn, mp // tm
    assert nsamp <= tm and nsamp % SUBLANES == 0
    n_sub = 4 if tm % (4 * SUBLANES * 2) == 0 else 1
    body = functools.partial(_ffn_body, tiles_per_seq=seq // tm, n_prompt_tiles=ni, n_sub=n_sub)
    return pl.pallas_call(
        body,
        grid=(nj, ni + 1),
        in_specs=[
            pl.BlockSpec((tm, kd), lambda j, i: (i, 0)),
            pl.BlockSpec((None, kd, tn), lambda j, i: (layer, 0, j)),
            pl.BlockSpec((None, kd, tn), lambda j, i: (layer, 0, nj + j)),
            pl.BlockSpec((kc, tn), lambda j, i: (0, j)),
            pl.BlockSpec((1, tn), lambda j, i: (0, j)),
            pl.BlockSpec((kc - 1, nsamp, tn), lambda j, i: (0, 0, j)),
        ],
        out_specs=[
            pl.BlockSpec((tm, tn), lambda j, i: (i, j)),
            pl.BlockSpec((SUBLANES, tn), lambda j, i: (i, j)),
            pl.BlockSpec((nsamp, tn), lambda j, i: (0, j)),
        ],
        out_shape=[
            jax.ShapeDtypeStruct((m_total, dff), BF16),
            jax.ShapeDtypeStruct(((ni + 1) * SUBLANES, dff), F32),
            jax.ShapeDtypeStruct((nsamp, dff), F32),
        ],
        scratch_shapes=[pltpu.VMEM((SUBLANES, tn), F32), pltpu.VMEM((kd, tn), BF16), pltpu.VMEM((kd, tn), BF16)],
        compiler_params=_params("parallel", "arbitrary"),
        name="ffn_in",
    )(h, w_in, w_in, conv_w, conv_b.reshape(1, dff), conv_state)


def _new_conv_state(old_state, pre_rows):
    return jnp.concatenate([old_state[:, 1:], pre_rows[:, None, :]], axis=1)


def _prompt_conv_tail(proj, batch, seq, keep, cols):
    return jnp.stack([lax.slice(proj, ((b + 1) * seq - keep, 0), ((b + 1) * seq, cols)) for b in range(batch)])


def _sample_rows(proj, mp, cols):
    return lax.slice(proj, (mp, 0), (proj.shape[0], cols))


def _gdn_gate_weights(w_in_t, n_heads, first_row):
    n_layers, d_model = w_in_t.shape[0], w_in_t.shape[2]
    tail = lax.slice(w_in_t, (0, first_row, 0), (n_layers, first_row + 2 * n_heads, d_model))
    zeros = jnp.zeros((n_layers, LANES - n_heads, d_model), w_in_t.dtype)
    return jnp.concatenate([tail[:, :n_heads], zeros, tail[:, n_heads:], zeros], axis=1)


def _gdn_layer(x, h, mp, batch, seq, j, s_all, s_new_all, conv_state, w_in_t, w_ab_t, conv_w, a_log, dt_bias,
               norm_w, w_out):
    n_heads = a_log.shape[0]
    m_total = x.shape[0]
    vd = w_out.shape[1]
    cd = conv_w.shape[1]
    nsamp = s_all.shape[1]
    assert vd == n_heads * LANES and cd == 3 * vd, "kernels assume DK == DV == 128"
    kc = conv_w.shape[0]

    pad = lambda p: jnp.zeros((1, LANES), F32).at[0, :n_heads].set(p.astype(F32))
    expand = (jnp.arange(LANES)[:, None] == (jnp.arange(vd) // LANES)[None, :]).astype(BF16)

    proj = _matmul(h, w_in_t, j, n_cols=cd + vd, tn=512, w_rows_are_outputs=True)
    ab = _matmul(h, w_ab_t, j, w_rows_are_outputs=True)

    gcb, beb, gc_t = _gdn_gates(ab, pad(a_log), pad(dt_bias), expand, 0, mp, True)
    o, s_p = _gdn_prompt(proj, gcb, gc_t, beb, conv_w, norm_w, m_total, batch, seq, n_heads,
                         rows_per_step=min(seq, 256), heads_per_step=min(n_heads, 4))

    gb_s, beb_s = _gdn_gates(ab, pad(a_log), pad(dt_bias), expand, mp, nsamp, False)
    state_t = jnp.transpose(conv_state, (1, 0, 2))
    o, s_new_all = _gdn_sample(proj, state_t, gb_s, beb_s, conv_w, norm_w, s_all, j, o, s_new_all, mp,
                               n_heads, samples_per_step=min(nsamp, 64))

    x = _matmul(o, w_out, j, residual=x, tn=512)

    conv_p = _prompt_conv_tail(proj, batch, seq, kc - 1, cd)
    conv_s = _new_conv_state(conv_state, _sample_rows(proj, mp, cd))
    return x, s_p, s_new_all, conv_p, conv_s


def _lru_layer(x, h, mp, batch, seq, j, h0, conv_state, w_in, b_in, conv_w, conv_b, w_gate_bf, b_gate, lam,
               w_out):
    m_total = x.shape[0]
    drnn = conv_w.shape[1]
    kc = conv_w.shape[0]
    proj = _matmul(h, w_in, j, bias=b_in, tn=512)
    o, h_p = _lru_prompt(proj, conv_w, conv_b, w_gate_bf, b_gate, lam, m_total, batch, seq,
                         rows_per_step=min(seq, 512))
    state_t = jnp.transpose(conv_state, (1, 0, 2))
    o, h_s = _lru_sample(proj, state_t, h0, conv_w, conv_b, w_gate_bf, b_gate, lam, o, mp)
    x = _matmul(o, w_out, j, residual=x, tn=512)
    conv_p = _prompt_conv_tail(proj, batch, seq, kc - 1, drnn)
    conv_s = _new_conv_state(conv_state, _sample_rows(proj, mp, drnn))
    return x, h_p.reshape(batch, drnn), h_s, conv_p, conv_s


def _ffn_layer(x, h, mp, batch, seq, i, conv_state, w_in, conv_w, conv_b, w_out):
    m_total = x.shape[0]
    dff = conv_w.shape[1]
    kc = conv_w.shape[0]
    state_t = jnp.transpose(conv_state, (1, 0, 2))
    act, tails, gate_s = _ffn_in(h, w_in, i, conv_w, conv_b, state_t, batch, seq)
    x = _matmul(act, w_out, i, residual=x, tm=416, tn=512, tk=dff)
    n_tiles = tails.shape[0] // SUBLANES - 1
    tails = tails[:n_tiles * SUBLANES].reshape(batch, -1, SUBLANES, dff)
    conv_p = tails[:, -1, SUBLANES - (kc - 1):, :]
    conv_s = _new_conv_state(conv_state, gate_s)
    return x, conv_p, conv_s


def kernel(x_prompt, x_sample, state_gdn_S, state_gdn_conv, state_lru_h, state_lru_conv, state_ffn_conv, norm_mixer, norm_ffn, norm_final, gdn_w_in, gdn_conv_w, gdn_A_log, gdn_dt_bias, gdn_norm_w, gdn_w_out, lru_w_in, lru_b_in, lru_conv_w, lru_conv_b, lru_w_gate, lru_b_gate, lru_lambda, lru_w_out, ffn_w_in, ffn_conv_w, ffn_conv_b, ffn_w_out):
    batch, seq, d_model = x_prompt.shape
    nsamp = x_sample.shape[0]
    depth = norm_mixer.shape[0]
    mp = batch * seq
    x = jnp.concatenate([x_prompt.reshape(mp, d_model), x_sample.reshape(nsamp, d_model)], axis=0)

    n_heads = gdn_A_log.shape[1]
    gdn_w_in_t = jnp.swapaxes(gdn_w_in, 1, 2)
    gdn_w_ab_t = _gdn_gate_weights(gdn_w_in_t, n_heads, gdn_w_in.shape[2] - 2 * n_heads)
    lru_w_gate_bf = lru_w_gate.astype(BF16)
    ffn_w_out = ffn_w_out.astype(BF16)

    gs_p, gc_p, gc_s = [], [], []
    gs_s = None
    lh_p, lh_s, lc_p, lc_s = [], [], [], []
    fc_p, fc_s = [], []
    for i in range(depth):
        j = i // 2
        h = _rmsnorm(x, norm_mixer[i], BF16)
        if i % 2 == 0:
            x, sp, gs_s, cp, cs = _gdn_layer(
                x, h, mp, batch, seq, j, state_gdn_S, gs_s, state_gdn_conv[j], gdn_w_in_t, gdn_w_ab_t,
                gdn_conv_w[j], gdn_A_log[j], gdn_dt_bias[j], gdn_norm_w[j], gdn_w_out)
            gs_p.append(sp); gc_p.append(cp); gc_s.append(cs)
        else:
            x, hp, hs, cp, cs = _lru_layer(
                x, h, mp, batch, seq, j, state_lru_h[j], state_lru_conv[j], lru_w_in, lru_b_in[j],
                lru_conv_w[j], lru_conv_b[j], lru_w_gate_bf[j], lru_b_gate[j], lru_lambda[j], lru_w_out)
            lh_p.append(hp); lh_s.append(hs); lc_p.append(cp); lc_s.append(cs)
        h = _rmsnorm(x, norm_ffn[i], BF16)
        x, cp, cs = _ffn_layer(x, h, mp, batch, seq, i, state_ffn_conv[i], ffn_w_in, ffn_conv_w[i],
                               ffn_conv_b[i], ffn_w_out)
        fc_p.append(cp); fc_s.append(cs)

    y_prompt = _rmsnorm(x, norm_final, F32, 0, mp).reshape(batch, seq, d_model)
    y_sample = _rmsnorm(x, norm_final, F32, mp, nsamp).reshape(nsamp, 1, d_model)
    return (y_prompt, y_sample,
            jnp.stack(gs_p), gs_s, jnp.stack(gc_p), jnp.stack(gc_s),
            jnp.stack(lh_p), jnp.stack(lh_s), jnp.stack(lc_p), jnp.stack(lc_s),
            jnp.stack(fc_p), jnp.stack(fc_s))
```

```python
import functools
import math

import jax
import jax.numpy as jnp
from jax import lax
from jax.experimental import pallas as pl
from jax.experimental.pallas import tpu as pltpu

F32 = jnp.float32
BF16 = jnp.bfloat16
HIGHEST = lax.Precision.HIGHEST

NORM_EPS = 1e-6
L2_EPS = 1e-6
LRU_C = 8.0
GDN_CHUNK = 64

LANES = 128
SUBLANES = 8
VMEM_LIMIT_BYTES = 56 * 1024 * 1024


def _params(*sem):
    return pltpu.CompilerParams(dimension_semantics=sem, vmem_limit_bytes=VMEM_LIMIT_BYTES)


def _pick(n, target, align):
    best = None
    t = align
    while t <= min(n, target):
        if n % t == 0:
            best = t
        t += align
    return best if best is not None else n


def _dot(a, b, precision=None):
    return jnp.dot(a, b, preferred_element_type=F32, precision=precision)


def _dot_nt(a, b, precision=None):
    return lax.dot_general(a, b, (((1,), (1,)), ((), ())), preferred_element_type=F32, precision=precision)


def _softplus(x):
    return jnp.maximum(x, 0.0) + jnp.log1p(jnp.exp(-jnp.abs(x)))


def _sigmoid(x):
    return 1.0 / (1.0 + jnp.exp(-x))


def _silu(x):
    return x * _sigmoid(x)


def _gelu_tanh(x):
    c = math.sqrt(2.0 / math.pi)
    return 0.5 * x * (1.0 + jnp.tanh(c * (x + 0.044715 * (x * x * x))))


def _gelu_erf(x):
    return 0.5 * x * (1.0 + lax.erf(x * (1.0 / math.sqrt(2.0))))


def _rmsnorm_body(x_ref, w_ref, o_ref):
    x = x_ref[...]
    y = x * lax.rsqrt(jnp.mean(x * x, axis=-1, keepdims=True) + NORM_EPS)
    o_ref[...] = (y * w_ref[...]).astype(o_ref.dtype)


def _rmsnorm(x, w, out_dtype, row0=0, rows=None):
    d = x.shape[1]
    rows = x.shape[0] if rows is None else rows
    tr = _pick(math.gcd(rows, row0) if row0 else rows, 512, 16)
    rb0 = row0 // tr
    return pl.pallas_call(
        _rmsnorm_body,
        grid=(rows // tr,),
        in_specs=[pl.BlockSpec((tr, d), lambda i: (rb0 + i, 0)), pl.BlockSpec((1, d), lambda i: (0, 0))],
        out_specs=pl.BlockSpec((tr, d), lambda i: (i, 0)),
        out_shape=jax.ShapeDtypeStruct((rows, d), out_dtype),
        compiler_params=_params("parallel"),
        name="rmsnorm",
    )(x, w.reshape(1, d))


def _matmul_body(*refs, nk, has_bias, has_res, cast_w, w_rows_are_outputs):
    x_ref, w_ref = refs[0], refs[1]
    pos = 2
    b_ref = r_ref = None
    if has_bias:
        b_ref = refs[pos]
        pos += 1
    if has_res:
        r_ref = refs[pos]
        pos += 1
    o_ref = refs[pos]

    def finish(acc):
        if has_bias:
            acc = acc + b_ref[...]
        if has_res:
            acc = r_ref[...] + acc
        o_ref[...] = acc.astype(o_ref.dtype)

    if cast_w:
        wb_ref = refs[pos + 1]

        @pl.when(pl.program_id(1) == 0)
        def _():
            wb_ref[...] = w_ref[...].astype(BF16)

        finish((_dot_nt if w_rows_are_outputs else _dot)(x_ref[...], wb_ref[...]))
        return

    part = _dot(x_ref[...], w_ref[...])
    if nk == 1:
        finish(part)
    else:
        acc_ref = refs[pos + 1]
        k = pl.program_id(2)

        @pl.when(k == 0)
        def _():
            acc_ref[...] = part

        @pl.when(k > 0)
        def _():
            acc_ref[...] += part

        @pl.when(k == nk - 1)
        def _():
            finish(acc_ref[...])


def _matmul(x, w, layer, bias=None, residual=None, n_cols=None, out_dtype=F32, tm=832, tn=1024, tk=4096,
            row0=0, rows=None, w_rows_are_outputs=False):
    kd = x.shape[1]
    m = x.shape[0] if rows is None else rows
    n = (w.shape[1] if w_rows_are_outputs else w.shape[2]) if n_cols is None else n_cols
    cast_w = w.dtype != BF16
    assert cast_w or not w_rows_are_outputs
    tm = _pick(math.gcd(m, row0) if row0 else m, tm, 16)
    tn = _pick(n, tn, LANES)
    tk = kd if cast_w else _pick(kd, tk, LANES)
    nk = kd // tk
    rb0 = row0 // tm
    assert residual is None or row0 == 0
    if w_rows_are_outputs:
        w_spec = pl.BlockSpec((None, tn, tk), lambda j, i, k: (layer, j, k))
    else:
        w_spec = pl.BlockSpec((None, tk, tn), lambda j, i, k: (layer, k, j))
    in_specs = [pl.BlockSpec((tm, tk), lambda j, i, k: (rb0 + i, k)), w_spec]
    args = [x, w]
    if bias is not None:
        in_specs.append(pl.BlockSpec((1, tn), lambda j, i, k: (0, j)))
        args.append(bias.reshape(1, n).astype(F32))
    if residual is not None:
        in_specs.append(pl.BlockSpec((tm, tn), lambda j, i, k: (i, j)))
        args.append(residual)
    body = functools.partial(_matmul_body, nk=nk, has_bias=bias is not None, has_res=residual is not None,
                             cast_w=cast_w, w_rows_are_outputs=w_rows_are_outputs)
    if cast_w:
        scratch = [pltpu.VMEM((tn, tk) if w_rows_are_outputs else (tk, tn), BF16)]
    else:
        scratch = [pltpu.VMEM((tm, tn), F32)] if nk > 1 else []
    return pl.pallas_call(
        body,
        grid=(n // tn, m // tm, nk),
        in_specs=in_specs,
        out_specs=pl.BlockSpec((tm, tn), lambda j, i, k: (i, j)),
        out_shape=jax.ShapeDtypeStruct((m, n), out_dtype),
        scratch_shapes=scratch,
        compiler_params=_params("parallel", "arbitrary", "arbitrary"),
        name="matmul",
    )(*args)


def _shift_rows(x, prev8, s):
    if s == 0:
        return x
    xs = pltpu.roll(x, s, 0)
    row = lax.broadcasted_iota(jnp.int32, prev8.shape, 0)
    top = jnp.where(row < s, pltpu.roll(prev8, s, 0), xs[:SUBLANES])
    if x.shape[0] == SUBLANES:
        return top
    return jnp.concatenate([top, xs[SUBLANES:]], axis=0)


def _causal_conv_rows(x, prev8, w):
    kc = w.shape[0]
    y = None
    for kk in range(kc):
        term = _shift_rows(x, prev8, kc - 1 - kk) * w[kk:kk + 1]
        y = term if y is None else y + term
    return y


def _conv_from_state(state_ref, x, w):
    kc = w.shape[0]
    y = state_ref[0] * w[0:1]
    for kk in range(1, kc - 1):
        y = y + state_ref[kk] * w[kk:kk + 1]
    return y + x * w[kc - 1:kc]


def _expand_lanes(x, e):
    hi = x.astype(BF16)
    rest = x - hi.astype(F32)
    mid = rest.astype(BF16)
    lo = (rest - mid.astype(F32)).astype(BF16)
    return _dot(hi, e) + _dot(mid, e) + _dot(lo, e)


def _gdn_gate_body(a_ref, b_ref, alog_ref, dtb_ref, e_ref, *out_refs, chunk, cumulative):
    g = -jnp.exp(alog_ref[...]) * _softplus(a_ref[...] + dtb_ref[...])
    beta = _sigmoid(b_ref[...])
    e = e_ref[...]
    if cumulative:
        r = g.shape[0]
        ri = lax.broadcasted_iota(jnp.int32, (chunk, chunk), 0)
        ci = lax.broadcasted_iota(jnp.int32, (chunk, chunk), 1)
        tril = (ri >= ci).astype(F32)
        gc = jnp.concatenate(
            [_dot(tril, g[c * chunk:(c + 1) * chunk], HIGHEST) for c in range(r // chunk)], axis=0)
        out_refs[0][...] = _expand_lanes(gc, e)
        out_refs[1][...] = _expand_lanes(beta, e)
        out_refs[2][...] = jnp.transpose(gc)
    else:
        out_refs[0][...] = _expand_lanes(g, e)
        out_refs[1][...] = _expand_lanes(beta, e)


def _gdn_gates(ab, alog_pad, dtb_pad, expand, row0, rows, cumulative):
    hd = expand.shape[1]
    tr = _pick(rows, 256, LANES)
    cw = _pick(hd, 1024, LANES)
    rb0 = row0 // tr
    assert row0 % tr == 0 and tr % GDN_CHUNK == 0
    wide = pl.BlockSpec((tr, cw), lambda i, j: (i, j))
    out_specs = [wide, wide]
    out_shape = [jax.ShapeDtypeStruct((rows, hd), F32)] * 2
    if cumulative:
        out_specs.append(pl.BlockSpec((LANES, tr), lambda i, j: (0, i)))
        out_shape.append(jax.ShapeDtypeStruct((LANES, rows), F32))
    body = functools.partial(_gdn_gate_body, chunk=GDN_CHUNK, cumulative=cumulative)
    return pl.pallas_call(
        body,
        grid=(rows // tr, hd // cw),
        in_specs=[
            pl.BlockSpec((tr, LANES), lambda i, j: (rb0 + i, 0)),
            pl.BlockSpec((tr, LANES), lambda i, j: (rb0 + i, 1)),
            pl.BlockSpec((1, LANES), lambda i, j: (0, 0)),
            pl.BlockSpec((1, LANES), lambda i, j: (0, 0)),
            pl.BlockSpec((LANES, cw), lambda i, j: (0, j)),
        ],
        out_specs=out_specs,
        out_shape=out_shape,
        compiler_params=_params("parallel", "arbitrary"),
        name="gdn_gates",
    )(ab, ab, alog_pad, dtb_pad, expand)


def _gated_head_norm(o, z, nw):
    o = o * lax.rsqrt(jnp.mean(o * o, axis=-1, keepdims=True) + NORM_EPS) * nw
    return o * _silu(z)


def _gdn_block_masks(r, chunk, dk):
    ri = lax.broadcasted_iota(jnp.int32, (r, r), 0)
    ci = lax.broadcasted_iota(jnp.int32, (r, r), 1)
    first = (ri // chunk) * chunk
    causal = (ci >= first) & (ci <= ri)
    strict = (ci >= first) & (ci < ri)
    eye = (ri == ci).astype(F32)
    bi = lax.broadcasted_iota(jnp.int32, (r // chunk * dk, r), 0) // dk
    bj = lax.broadcasted_iota(jnp.int32, (r // chunk * dk, r), 1) // chunk
    return causal, strict, eye, bi == bj


def _gdn_block_terms(qs, ks, vs, gcbs, gc_rows, bebs, chunk, masks):
    nh = len(qs)
    hs = range(nh)
    r = ks[0].shape[0]
    dv = vs[0].shape[1]
    nch = r // chunk
    causal, strict, eye, kt_mask = masks
    decay = [jnp.where(causal, jnp.exp(jnp.where(
        causal, jnp.concatenate([gcbs[h]] * (r // LANES), axis=1) - gc_rows[h], 0.0)), 0.0) for h in hs]
    eg = [jnp.exp(gcbs[h]) for h in hs]
    kb = [ks[h] * bebs[h] for h in hs]
    kq = [_dot_nt(jnp.concatenate([kb[h], qs[h]], axis=0).astype(BF16), ks[h].astype(BF16)) for h in hs]
    qk = [kq[h][r:] * decay[h] for h in hs]
    n = [-jnp.where(strict, kq[h][:r] * decay[h], 0.0) for h in hs]
    tm = [eye + n[h] for h in hs]
    nb = [n[h].astype(BF16) for h in hs]
    p = [_dot(nb[h], nb[h]) for h in hs]
    steps = int(math.log2(chunk)) - 1
    for i in range(steps):
        pb = [p[h].astype(BF16) for h in hs]
        if i < steps - 1:
            both = [_dot(jnp.concatenate([pb[h], tm[h].astype(BF16)], axis=0), pb[h]) for h in hs]
            p = [both[h][:r] for h in hs]
            tm = [tm[h] + both[h][r:] for h in hs]
        else:
            tm = [tm[h] + _dot(tm[h].astype(BF16), pb[h]) for h in hs]
    uwb = [_dot(tm[h].astype(BF16), jnp.concatenate([vs[h] * bebs[h], kb[h] * eg[h]], axis=1).astype(BF16)
                ).astype(BF16) for h in hs]
    gl = [jnp.concatenate(
        [jnp.broadcast_to(gcbs[h][(c + 1) * chunk - 1:(c + 1) * chunk, :], (chunk, LANES)) for c in range(nch)],
        axis=0) for h in hs]
    k_tail_t = [jnp.transpose(ks[h] * jnp.exp(gl[h] - gcbs[h])) for h in hs]
    kt_blocks = [jnp.where(kt_mask, jnp.concatenate([k_tail_t[h]] * nch, axis=0), 0.0) for h in hs]
    both = [_dot(jnp.concatenate([qk[h], kt_blocks[h]], axis=0).astype(BF16), uwb[h]) for h in hs]
    return [(qs[h] * eg[h] - both[h][:r, dv:], both[h][:r, :dv], both[h][r:, dv:], both[h][r:, :dv], gl[h])
            for h in hs]


def _gdn_prompt_body(q_ref, k_ref, v_ref, z_ref, gc_ref, gr_ref, be_ref, wq_ref, wk_ref, wv_ref, nw_ref,
                     o_init_ref, o_ref, s_out_ref, s_scr, carry_scr, *, heads, chunk, q_scale):
    del o_init_ref
    t = pl.program_id(2)

    @pl.when(t == 0)
    def _():
        s_scr[...] = jnp.zeros_like(s_scr)
        carry_scr[...] = jnp.zeros_like(carry_scr)

    r = q_ref.shape[0]

    def conv_silu(x_ref, w_ref, idx):
        x = x_ref[...]
        y = _causal_conv_rows(x, carry_scr[idx], w_ref[...])
        carry_scr[idx] = x[r - SUBLANES:]
        return _silu(y)

    qc = conv_silu(q_ref, wq_ref, 0)
    kc = conv_silu(k_ref, wk_ref, 1)
    vc = conv_silu(v_ref, wv_ref, 2)

    nw = nw_ref[...]
    dk = LANES
    masks = _gdn_block_masks(r, chunk, dk)

    lanes = [slice(h * LANES, (h + 1) * LANES) for h in range(heads)]
    qs, ks = [], []
    for ls in lanes:
        qh = qc[:, ls]
        kh = kc[:, ls]
        qs.append(qh * lax.rsqrt(jnp.sum(qh * qh, axis=-1, keepdims=True) + L2_EPS) * q_scale)
        ks.append(kh * lax.rsqrt(jnp.sum(kh * kh, axis=-1, keepdims=True) + L2_EPS))
    terms = _gdn_block_terms(qs, ks, [vc[:, ls] for ls in lanes], [gc_ref[:, ls] for ls in lanes],
                             [gr_ref[h, 0] for h in range(heads)], [be_ref[:, ls] for ls in lanes], chunk, masks)
    s = [s_scr[h] for h in range(heads)]
    for c in range(r // chunk):
        rs = slice(c * chunk, (c + 1) * chunk)
        bs = slice(c * dk, (c + 1) * dk)
        for h, ls in enumerate(lanes):
            q_eff, o0, m_mat, b_mat, gl = terms[h]
            lhs = jnp.concatenate([q_eff[rs], m_mat[bs]], axis=0).astype(BF16)
            ls_s = _dot(lhs, s[h].astype(BF16))
            o = ls_s[:chunk] + o0[rs]
            s_decay = jnp.broadcast_to(jnp.exp(gl[c * chunk:c * chunk + 1, :]), s[h].shape)
            s[h] = s[h] * s_decay - ls_s[chunk:] + b_mat[bs]
            o_ref[rs, ls] = _gated_head_norm(o, z_ref[rs, ls], nw).astype(o_ref.dtype)
    for h in range(heads):
        s_scr[h] = s[h]
        s_out_ref[0, h] = s[h]


def _gdn_prompt(proj, gcb, gc_t, beb, conv_w, norm_w, m_total, batch, seq, n_heads, rows_per_step,
                heads_per_step):
    hd = n_heads * LANES
    g = heads_per_step
    gw = g * LANES
    r = rows_per_step
    nt = seq // r
    ncb = hd // gw
    assert r % LANES == 0 and r % GDN_CHUNK == 0
    gc_rows = gc_t.reshape(LANES, (batch * seq) // r, 1, r)
    row = lambda b, hg, t: b * nt + t
    sect = lambda s: pl.BlockSpec((r, gw), lambda b, hg, t: (row(b, hg, t), s * ncb + hg))
    gate = pl.BlockSpec((r, gw), lambda b, hg, t: (row(b, hg, t), hg))
    wsp = lambda s: pl.BlockSpec((conv_w.shape[0], gw), lambda b, hg, t: (0, s * ncb + hg))
    body = functools.partial(_gdn_prompt_body, heads=g, chunk=GDN_CHUNK, q_scale=float(LANES) ** -0.5)
    return pl.pallas_call(
        body,
        grid=(batch, n_heads // g, nt),
        in_specs=[sect(0), sect(1), sect(2), sect(3), gate,
                  pl.BlockSpec((g, 1, 1, r), lambda b, hg, t: (hg, row(b, hg, t), 0, 0)),
                  gate, wsp(0), wsp(1), wsp(2),
                  pl.BlockSpec((1, LANES), lambda b, hg, t: (0, 0)),
                  pl.BlockSpec(memory_space=pl.ANY)],
        out_specs=[
            pl.BlockSpec((r, gw), lambda b, hg, t: (row(b, hg, t), hg)),
            pl.BlockSpec((1, g, LANES, LANES), lambda b, hg, t: (b, hg, 0, 0)),
        ],
        out_shape=[
            jax.ShapeDtypeStruct((m_total, hd), BF16),
            jax.ShapeDtypeStruct((batch, n_heads, LANES, LANES), F32),
        ],
        input_output_aliases={11: 0},
        scratch_shapes=[pltpu.VMEM((g, LANES, LANES), F32), pltpu.VMEM((3, SUBLANES, gw), F32)],
        compiler_params=_params("parallel", "parallel", "arbitrary"),
        name="gdn_prompt",
    )(proj, proj, proj, proj, gcb, gc_rows, beb, conv_w, conv_w, conv_w, norm_w.reshape(1, LANES),
      jnp.zeros((m_total, hd), BF16))


def _gdn_sample_body(q_ref, k_ref, v_ref, z_ref, sq_ref, sk_ref, sv_ref, g_ref, be_ref, wq_ref, wk_ref, wv_ref,
                     nw_ref, s_ref, *rest, q_scale):
    o_ref, s_out_ref, q_scr, k_scr, d_scr, o_scr = rest[-6:]
    nb = q_ref.shape[0]
    q = _silu(_conv_from_state(sq_ref, q_ref[...], wq_ref[...]))
    k = _silu(_conv_from_state(sk_ref, k_ref[...], wk_ref[...]))
    v = _silu(_conv_from_state(sv_ref, v_ref[...], wv_ref[...]))
    q_scr[...] = q * lax.rsqrt(jnp.sum(q * q, axis=-1, keepdims=True) + L2_EPS) * q_scale
    k_scr[...] = k * lax.rsqrt(jnp.sum(k * k, axis=-1, keepdims=True) + L2_EPS)
    d_scr[0] = jnp.exp(g_ref[...])
    d_scr[1] = be_ref[...]
    d_scr[2] = v

    def step(b, carry):
        row = pl.ds(b, 1)
        k_col = jnp.transpose(jnp.broadcast_to(k_scr[row, :], (LANES, LANES)))
        q_col = jnp.transpose(jnp.broadcast_to(q_scr[row, :], (LANES, LANES)))
        s = s_ref[b] * d_scr[0, row, :]
        ks = jnp.sum(k_col * s, axis=0, keepdims=True)
        delta = d_scr[1, row, :] * (d_scr[2, row, :] - ks)
        s = s + k_col * delta
        s_out_ref[b] = s
        o_scr[row, :] = jnp.sum(q_col * s, axis=0, keepdims=True)
        return carry

    lax.fori_loop(0, nb, step, 0, unroll=4)
    o_ref[...] = _gated_head_norm(o_scr[...], z_ref[...], nw_ref[...]).astype(o_ref.dtype)


def _gdn_sample(proj, conv_state, gb, beb, conv_w, norm_w, s_all, layer, o_full, s_new_all, row0, n_heads,
                samples_per_step):
    nsamp = s_all.shape[1]
    bs = samples_per_step
    assert row0 % bs == 0 and nsamp % bs == 0
    rb0 = row0 // bs
    nh = n_heads
    kc = conv_w.shape[0]
    sect = lambda s: pl.BlockSpec((bs, LANES), lambda h, i: (rb0 + i, s * nh + h))
    stat = lambda s: pl.BlockSpec((kc - 1, bs, LANES), lambda h, i: (0, i, s * nh + h))
    gate = pl.BlockSpec((bs, LANES), lambda h, i: (i, h))
    wsp = lambda s: pl.BlockSpec((kc, LANES), lambda h, i: (0, s * nh + h))
    sspec = pl.BlockSpec((None, bs, None, LANES, LANES), lambda h, i: (layer, i, h, 0, 0))
    untouched = pl.BlockSpec(memory_space=pl.ANY)
    in_specs = [sect(0), sect(1), sect(2), sect(3), stat(0), stat(1), stat(2), gate, gate,
                wsp(0), wsp(1), wsp(2), pl.BlockSpec((1, LANES), lambda h, i: (0, 0)), sspec, untouched]
    args = [proj, proj, proj, proj, conv_state, conv_state, conv_state, gb, beb, conv_w, conv_w, conv_w,
            norm_w.reshape(1, LANES), s_all, o_full]
    aliases = {len(args) - 1: 0}
    if s_new_all is not None:
        in_specs.append(untouched)
        args.append(s_new_all)
        aliases[len(args) - 1] = 1
    body = functools.partial(_gdn_sample_body, q_scale=float(LANES) ** -0.5)
    return pl.pallas_call(
        body,
        grid=(nh, nsamp // bs),
        in_specs=in_specs,
        out_specs=[pl.BlockSpec((bs, LANES), lambda h, i: (rb0 + i, h)), sspec],
        out_shape=[
            jax.ShapeDtypeStruct(o_full.shape, o_full.dtype),
            jax.ShapeDtypeStruct(s_all.shape, F32),
        ],
        input_output_aliases=aliases,
        scratch_shapes=[pltpu.VMEM((bs, LANES), F32), pltpu.VMEM((bs, LANES), F32),
                        pltpu.VMEM((3, bs, LANES), F32), pltpu.VMEM((bs, LANES), F32)],
        compiler_params=_params("parallel", "parallel"),
        name="gdn_sample",
    )(*args)


def _lru_coeffs(xc, wg_ref, bgr_ref, bgi_ref, lam_ref):
    bw = xc.shape[1]
    gates = _dot(xc.astype(BF16), wg_ref[0])
    r = _sigmoid(gates[:, :bw] + bgr_ref[...])
    i = _sigmoid(gates[:, bw:] + bgi_ref[...])
    log_a = -LRU_C * r * _softplus(-lam_ref[...])
    a = jnp.exp(log_a)
    mult = jnp.sqrt(-jnp.tanh(log_a) * (a * a + 1.0))
    return a, mult * (i * xc)


def _scan8(a, b):
    row = lax.broadcasted_iota(jnp.int32, a.shape, 0)
    for s in (1, 2, 4):
        a_sh = pltpu.roll(a, s, 0)
        b_sh = pltpu.roll(b, s, 0)
        m = row >= s
        b = jnp.where(m, a * b_sh + b, b)
        a = jnp.where(m, a * a_sh, a)
    return a, b


def _lru_prompt_body(x_ref, y_ref, cw_ref, cb_ref, wg_ref, bgr_ref, bgi_ref, lam_ref, o_init_ref,
                     o_ref, h_out_ref, carry_scr, h_scr, a_scr, b_scr):
    del o_init_ref
    t = pl.program_id(2)

    @pl.when(t == 0)
    def _():
        carry_scr[...] = jnp.zeros_like(carry_scr)
        h_scr[...] = jnp.zeros_like(h_scr)

    x = x_ref[...]
    r = x.shape[0]
    xc = _causal_conv_rows(x, carry_scr[...], cw_ref[...]) + cb_ref[...]
    carry_scr[...] = x[r - SUBLANES:]
    a, b = _lru_coeffs(xc, wg_ref, bgr_ref, bgi_ref, lam_ref)
    a_scr[...] = a
    b_scr[...] = b

    def group(i, h_prev):
        rows = pl.ds(pl.multiple_of(i * SUBLANES, SUBLANES), SUBLANES)
        a8, b8 = _scan8(a_scr[rows, :], b_scr[rows, :])
        h8 = a8 * h_prev + b8
        b_scr[rows, :] = h8
        return jnp.broadcast_to(h8[SUBLANES - 1:SUBLANES, :], h8.shape)

    h_last = lax.fori_loop(0, r // SUBLANES, group, h_scr[...], unroll=8)
    h_scr[...] = h_last
    h_out_ref[0] = h_last[0:1, :]
    o_ref[...] = (b_scr[...] * _gelu_tanh(y_ref[...])).astype(o_ref.dtype)


def _lru_prompt(proj, conv_w, conv_b, w_gate, b_gate, lam, m_total, batch, seq, rows_per_step):
    nb, bw = w_gate.shape[0], w_gate.shape[1]
    drnn = nb * bw
    r = rows_per_step
    nt = seq // r
    row = lambda b, n, t: b * nt + t
    vec = lambda off: pl.BlockSpec((1, bw), lambda b, n, t: (0, off + n))
    return pl.pallas_call(
        _lru_prompt_body,
        grid=(batch, nb, nt),
        in_specs=[
            pl.BlockSpec((r, bw), lambda b, n, t: (row(b, n, t), n)),
            pl.BlockSpec((r, bw), lambda b, n, t: (row(b, n, t), nb + n)),
            pl.BlockSpec((conv_w.shape[0], bw), lambda b, n, t: (0, n)),
            vec(0),
            pl.BlockSpec((1, bw, 2 * bw), lambda b, n, t: (n, 0, 0)),
            vec(0), vec(nb), vec(0),
            pl.BlockSpec(memory_space=pl.ANY),
        ],
        out_specs=[
            pl.BlockSpec((r, bw), lambda b, n, t: (row(b, n, t), n)),
            pl.BlockSpec((1, 1, bw), lambda b, n, t: (b, 0, n)),
        ],
        out_shape=[
            jax.ShapeDtypeStruct((m_total, drnn), BF16),
            jax.ShapeDtypeStruct((batch, 1, drnn), F32),
        ],
        input_output_aliases={8: 0},
        scratch_shapes=[pltpu.VMEM((SUBLANES, bw), F32), pltpu.VMEM((SUBLANES, bw), F32),
                        pltpu.VMEM((r, bw), F32), pltpu.VMEM((r, bw), F32)],
        compiler_params=_params("parallel", "parallel", "arbitrary"),
        name="lru_prompt",
    )(proj, proj, conv_w, conv_b.reshape(1, drnn), w_gate, b_gate.reshape(1, 2 * drnn),
      b_gate.reshape(1, 2 * drnn), lam.reshape(1, drnn), jnp.zeros((m_total, drnn), BF16))


def _lru_sample_body(x_ref, y_ref, st_ref, h0_ref, cw_ref, cb_ref, wg_ref, bgr_ref, bgi_ref, lam_ref,
                     o_full_ref, o_ref, h_out_ref):
    del o_full_ref
    xc = _conv_from_state(st_ref, x_ref[...], cw_ref[...]) + cb_ref[...]
    a, b = _lru_coeffs(xc, wg_ref, bgr_ref, bgi_ref, lam_ref)
    h = a * h0_ref[...] + b
    h_out_ref[...] = h
    o_ref[...] = (h * _gelu_tanh(y_ref[...])).astype(o_ref.dtype)


def _lru_sample(proj, conv_state, h0, conv_w, conv_b, w_gate, b_gate, lam, o_full, row0):
    nsamp = h0.shape[0]
    nb, bw = w_gate.shape[0], w_gate.shape[1]
    drnn = nb * bw
    kc = conv_w.shape[0]
    assert row0 % nsamp == 0
    rb0 = row0 // nsamp
    vec = lambda off: pl.BlockSpec((1, bw), lambda n: (0, off + n))
    return pl.pallas_call(
        _lru_sample_body,
        grid=(nb,),
        in_specs=[
            pl.BlockSpec((nsamp, bw), lambda n: (rb0, n)),
            pl.BlockSpec((nsamp, bw), lambda n: (rb0, nb + n)),
            pl.BlockSpec((kc - 1, nsamp, bw), lambda n: (0, 0, n)),
            pl.BlockSpec((nsamp, bw), lambda n: (0, n)),
            pl.BlockSpec((kc, bw), lambda n: (0, n)),
            vec(0),
            pl.BlockSpec((1, bw, 2 * bw), lambda n: (n, 0, 0)),
            vec(0), vec(nb), vec(0),
            pl.BlockSpec(memory_space=pl.ANY),
        ],
        out_specs=[pl.BlockSpec((nsamp, bw), lambda n: (rb0, n)), pl.BlockSpec((nsamp, bw), lambda n: (0, n))],
        out_shape=[jax.ShapeDtypeStruct(o_full.shape, o_full.dtype), jax.ShapeDtypeStruct((nsamp, drnn), F32)],
        input_output_aliases={10: 0},
        compiler_params=_params("parallel"),
        name="lru_sample",
    )(proj, proj, conv_state, h0, conv_w, conv_b.reshape(1, drnn), w_gate, b_gate.reshape(1, 2 * drnn),
      b_gate.reshape(1, 2 * drnn), lam.reshape(1, drnn), o_full)


def _ffn_body(x_ref, wg32_ref, wv32_ref, cw_ref, cb_ref, st_ref, act_ref, tail_ref, gs_ref, carry_scr, wg_ref,
              wv_ref, *, tiles_per_seq, n_sub):
    i = pl.program_id(1)
    nsamp = gs_ref.shape[0]
    cw = cw_ref[...]
    cb = cb_ref[...]

    @pl.when(i == 0)
    def _():
        wg_ref[...] = wg32_ref[...].astype(BF16)
        wv_ref[...] = wv32_ref[...].astype(BF16)
        xs = x_ref[:nsamp, :]
        g = _dot(xs, wg_ref[...])
        v = _dot(xs, wv_ref[...])
        gate = _conv_from_state(st_ref, g, cw) + cb
        act_ref[:nsamp, :] = (_gelu_erf(gate) * v).astype(act_ref.dtype)
        gs_ref[...] = g
        tail_ref[...] = jnp.zeros_like(tail_ref)

    @pl.when((i - 1) % tiles_per_seq == 0)
    def _():
        carry_scr[...] = jnp.zeros_like(carry_scr)

    @pl.when(i > 0)
    def _():
        sub = x_ref.shape[0] // n_sub
        rows = [slice(s * sub, (s + 1) * sub) for s in range(n_sub)]

        def project(s):
            xs = x_ref[rows[s], :]
            return _dot(xs, wg_ref[...]), _dot(xs, wv_ref[...])

        prev8 = carry_scr[...]
        g, v = project(0)
        for s in range(n_sub):
            nxt = project(s + 1) if s + 1 < n_sub else None
            gate = _causal_conv_rows(g, prev8, cw) + cb
            act_ref[rows[s], :] = (_gelu_erf(gate) * v).astype(act_ref.dtype)
            prev8 = g[sub - SUBLANES:]
            if nxt is not None:
                g, v = nxt
        carry_scr[...] = prev8
        tail_ref[...] = prev8


def _ffn_in(h, w_in, layer, conv_w, conv_b, conv_state, batch, seq):
    m_total, kd = h.shape
    dff = conv_w.shape[1]
    kc = conv_w.shape[0]
    mp = batch * seq
    nsamp = m_total - mp
    tm = _pick(seq, 1024, 16)
    tn = _pick(dff, 256, LANES)
    nj, ni = dff // tn, mp // tm
    assert nsamp <= tm and nsamp % SUBLANES == 0
    n_sub = 4 if tm % (4 * SUBLANES * 2) == 0 else 1
    body = functools.partial(_ffn_body, tiles_per_seq=seq // tm, n_sub=n_sub)
    tile = lambda i: (i + ni) % (ni + 1)
    return pl.pallas_call(
        body,
        grid=(nj, ni + 1),
        in_specs=[
            pl.BlockSpec((tm, kd), lambda j, i: (tile(i), 0)),
            pl.BlockSpec((None, kd, tn), lambda j, i: (layer, 0, j)),
            pl.BlockSpec((None, kd, tn), lambda j, i: (layer, 0, nj + j)),
            pl.BlockSpec((kc, tn), lambda j, i: (0, j)),
            pl.BlockSpec((1, tn), lambda j, i: (0, j)),
            pl.BlockSpec((kc - 1, nsamp, tn), lambda j, i: (0, 0, j)),
        ],
        out_specs=[
            pl.BlockSpec((tm, tn), lambda j, i: (tile(i), j)),
            pl.BlockSpec((SUBLANES, tn), lambda j, i: (tile(i), j)),
            pl.BlockSpec((nsamp, tn), lambda j, i: (0, j)),
        ],
        out_shape=[
            jax.ShapeDtypeStruct((m_total, dff), BF16),
            jax.ShapeDtypeStruct(((ni + 1) * SUBLANES, dff), F32),
            jax.ShapeDtypeStruct((nsamp, dff), F32),
        ],
        scratch_shapes=[pltpu.VMEM((SUBLANES, tn), F32), pltpu.VMEM((kd, tn), BF16), pltpu.VMEM((kd, tn), BF16)],
        compiler_params=_params("parallel", "arbitrary"),
        name="ffn_in",
    )(h, w_in, w_in, conv_w, conv_b.reshape(1, dff), conv_state)


def _new_conv_state(old_state, pre_rows):
    return jnp.concatenate([old_state[:, 1:], pre_rows[:, None, :]], axis=1)


def _prompt_conv_tail(proj, batch, seq, keep, cols):
    return jnp.stack([lax.slice(proj, ((b + 1) * seq - keep, 0), ((b + 1) * seq, cols)) for b in range(batch)])


def _sample_rows(proj, mp, cols):
    return lax.slice(proj, (mp, 0), (proj.shape[0], cols))


def _gdn_gate_weights(w_in_t, n_heads, first_row):
    n_layers, d_model = w_in_t.shape[0], w_in_t.shape[2]
    tail = lax.slice(w_in_t, (0, first_row, 0), (n_layers, first_row + 2 * n_heads, d_model))
    zeros = jnp.zeros((n_layers, LANES - n_heads, d_model), w_in_t.dtype)
    return jnp.concatenate([tail[:, :n_heads], zeros, tail[:, n_heads:], zeros], axis=1)


def _gdn_layer(x, h, mp, batch, seq, j, s_all, s_new_all, conv_state, w_in_t, w_ab_t, conv_w, a_log, dt_bias,
               norm_w, w_out):
    n_heads = a_log.shape[0]
    m_total = x.shape[0]
    vd = w_out.shape[1]
    cd = conv_w.shape[1]
    nsamp = s_all.shape[1]
    assert vd == n_heads * LANES and cd == 3 * vd, "kernels assume DK == DV == 128"
    kc = conv_w.shape[0]

    pad = lambda p: jnp.zeros((1, LANES), F32).at[0, :n_heads].set(p.astype(F32))
    expand = (jnp.arange(LANES)[:, None] == (jnp.arange(vd) // LANES)[None, :]).astype(BF16)

    proj = _matmul(h, w_in_t, j, n_cols=cd + vd, tm=416, tn=1024, w_rows_are_outputs=True)
    ab = _matmul(h, w_ab_t, j, w_rows_are_outputs=True)

    gcb, beb, gc_t = _gdn_gates(ab, pad(a_log), pad(dt_bias), expand, 0, mp, True)
    o, s_p = _gdn_prompt(proj, gcb, gc_t, beb, conv_w, norm_w, m_total, batch, seq, n_heads,
                         rows_per_step=min(seq, 256), heads_per_step=min(n_heads, 4))

    gb_s, beb_s = _gdn_gates(ab, pad(a_log), pad(dt_bias), expand, mp, nsamp, False)
    state_t = jnp.transpose(conv_state, (1, 0, 2))
    o, s_new_all = _gdn_sample(proj, state_t, gb_s, beb_s, conv_w, norm_w, s_all, j, o, s_new_all, mp,
                               n_heads, samples_per_step=min(nsamp, 64))

    x = _matmul(o, w_out, j, residual=x, tn=512)

    conv_p = _prompt_conv_tail(proj, batch, seq, kc - 1, cd)
    conv_s = _new_conv_state(conv_state, _sample_rows(proj, mp, cd))
    return x, s_p, s_new_all, conv_p, conv_s


def _lru_layer(x, h, mp, batch, seq, j, h0, conv_state, w_in, b_in, conv_w, conv_b, w_gate_bf, b_gate, lam,
               w_out):
    m_total = x.shape[0]
    drnn = conv_w.shape[1]
    kc = conv_w.shape[0]
    proj = _matmul(h, w_in, j, bias=b_in, tm=416, tn=1024)
    o, h_p = _lru_prompt(proj, conv_w, conv_b, w_gate_bf, b_gate, lam, m_total, batch, seq,
                         rows_per_step=min(seq, 512))
    state_t = jnp.transpose(conv_state, (1, 0, 2))
    o, h_s = _lru_sample(proj, state_t, h0, conv_w, conv_b, w_gate_bf, b_gate, lam, o, mp)
    x = _matmul(o, w_out, j, residual=x, tn=512)
    conv_p = _prompt_conv_tail(proj, batch, seq, kc - 1, drnn)
    conv_s = _new_conv_state(conv_state, _sample_rows(proj, mp, drnn))
    return x, h_p.reshape(batch, drnn), h_s, conv_p, conv_s


def _ffn_layer(x, h, mp, batch, seq, i, conv_state, w_in, conv_w, conv_b, w_out):
    m_total = x.shape[0]
    dff = conv_w.shape[1]
    kc = conv_w.shape[0]
    state_t = jnp.transpose(conv_state, (1, 0, 2))
    act, tails, gate_s = _ffn_in(h, w_in, i, conv_w, conv_b, state_t, batch, seq)
    x = _matmul(act, w_out, i, residual=x, tm=416, tn=512, tk=dff)
    n_tiles = tails.shape[0] // SUBLANES - 1
    tails = tails[:n_tiles * SUBLANES].reshape(batch, -1, SUBLANES, dff)
    conv_p = tails[:, -1, SUBLANES - (kc - 1):, :]
    conv_s = _new_conv_state(conv_state, gate_s)
    return x, conv_p, conv_s


def kernel(x_prompt, x_sample, state_gdn_S, state_gdn_conv, state_lru_h, state_lru_conv, state_ffn_conv, norm_mixer, norm_ffn, norm_final, gdn_w_in, gdn_conv_w, gdn_A_log, gdn_dt_bias, gdn_norm_w, gdn_w_out, lru_w_in, lru_b_in, lru_conv_w, lru_conv_b, lru_w_gate, lru_b_gate, lru_lambda, lru_w_out, ffn_w_in, ffn_conv_w, ffn_conv_b, ffn_w_out):
    batch, seq, d_model = x_prompt.shape
    nsamp = x_sample.shape[0]
    depth = norm_mixer.shape[0]
    mp = batch * seq
    x = jnp.concatenate([x_prompt.reshape(mp, d_model), x_sample.reshape(nsamp, d_model)], axis=0)

    n_heads = gdn_A_log.shape[1]
    gdn_w_in_t = jnp.swapaxes(gdn_w_in, 1, 2)
    gdn_w_ab_t = _gdn_gate_weights(gdn_w_in_t, n_heads, gdn_w_in.shape[2] - 2 * n_heads)
    lru_w_gate_bf = lru_w_gate.astype(BF16)
    ffn_w_out = ffn_w_out.astype(BF16)

    gs_p, gc_p, gc_s = [], [], []
    gs_s = None
    lh_p, lh_s, lc_p, lc_s = [], [], [], []
    fc_p, fc_s = [], []
    for i in range(depth):
        j = i // 2
        h = _rmsnorm(x, norm_mixer[i], BF16)
        if i % 2 == 0:
            x, sp, gs_s, cp, cs = _gdn_layer(
                x, h, mp, batch, seq, j, state_gdn_S, gs_s, state_gdn_conv[j], gdn_w_in_t, gdn_w_ab_t,
                gdn_conv_w[j], gdn_A_log[j], gdn_dt_bias[j], gdn_norm_w[j], gdn_w_out)
            gs_p.append(sp); gc_p.append(cp); gc_s.append(cs)
        else:
            x, hp, hs, cp, cs = _lru_layer(
                x, h, mp, batch, seq, j, state_lru_h[j], state_lru_conv[j], lru_w_in, lru_b_in[j],
                lru_conv_w[j], lru_conv_b[j], lru_w_gate_bf[j], lru_b_gate[j], lru_lambda[j], lru_w_out)
            lh_p.append(hp); lh_s.append(hs); lc_p.append(cp); lc_s.append(cs)
        h = _rmsnorm(x, norm_ffn[i], BF16)
        x, cp, cs = _ffn_layer(x, h, mp, batch, seq, i, state_ffn_conv[i], ffn_w_in, ffn_conv_w[i],
                               ffn_conv_b[i], ffn_w_out)
        fc_p.append(cp); fc_s.append(cs)

    y_prompt = _rmsnorm(x, norm_final, F32, 0, mp).reshape(batch, seq, d_model)
    y_sample = _rmsnorm(x, norm_final, F32, mp, nsamp).reshape(nsamp, 1, d_model)
    return (y_prompt, y_sample,
            jnp.stack(gs_p), gs_s, jnp.stack(gc_p), jnp.stack(gc_s),
            jnp.stack(lh_p), jnp.stack(lh_s), jnp.stack(lc_p), jnp.stack(lc_s),
            jnp.stack(fc_p), jnp.stack(fc_s))
```

```python
import functools
import math

import jax
import jax.numpy as jnp
from jax import lax
from jax.experimental import pallas as pl
from jax.experimental.pallas import tpu as pltpu

F32 = jnp.float32
BF16 = jnp.bfloat16
HIGHEST = lax.Precision.HIGHEST

NORM_EPS = 1e-6
L2_EPS = 1e-6
LRU_C = 8.0
GDN_CHUNK = 64

LANES = 128
SUBLANES = 8
VMEM_LIMIT_BYTES = 56 * 1024 * 1024


def _params(*sem):
    return pltpu.CompilerParams(dimension_semantics=sem, vmem_limit_bytes=VMEM_LIMIT_BYTES)


def _pick(n, target, align):
    best = None
    t = align
    while t <= min(n, target):
        if n % t == 0:
            best = t
        t += align
    return best if best is not None else n


def _dot(a, b, precision=None):
    return jnp.dot(a, b, preferred_element_type=F32, precision=precision)


def _dot_nt(a, b, precision=None):
    return lax.dot_general(a, b, (((1,), (1,)), ((), ())), preferred_element_type=F32, precision=precision)


def _softplus(x):
    return jnp.maximum(x, 0.0) + jnp.log1p(jnp.exp(-jnp.abs(x)))


def _sigmoid(x):
    return 1.0 / (1.0 + jnp.exp(-x))


def _silu(x):
    return x * _sigmoid(x)


def _gelu_tanh(x):
    c = math.sqrt(2.0 / math.pi)
    return 0.5 * x * (1.0 + jnp.tanh(c * (x + 0.044715 * (x * x * x))))


def _gelu_erf(x):
    return 0.5 * x * (1.0 + lax.erf(x * (1.0 / math.sqrt(2.0))))


def _rmsnorm_body(x_ref, w_ref, o_ref):
    x = x_ref[...]
    y = x * lax.rsqrt(jnp.mean(x * x, axis=-1, keepdims=True) + NORM_EPS)
    o_ref[...] = (y * w_ref[...]).astype(o_ref.dtype)


def _rmsnorm(x, w, out_dtype, row0=0, rows=None):
    d = x.shape[1]
    rows = x.shape[0] if rows is None else rows
    tr = _pick(math.gcd(rows, row0) if row0 else rows, 512, 16)
    rb0 = row0 // tr
    return pl.pallas_call(
        _rmsnorm_body,
        grid=(rows // tr,),
        in_specs=[pl.BlockSpec((tr, d), lambda i: (rb0 + i, 0)), pl.BlockSpec((1, d), lambda i: (0, 0))],
        out_specs=pl.BlockSpec((tr, d), lambda i: (i, 0)),
        out_shape=jax.ShapeDtypeStruct((rows, d), out_dtype),
        compiler_params=_params("parallel"),
        name="rmsnorm",
    )(x, w.reshape(1, d))


def _matmul_body(*refs, nk, has_bias, has_res, cast_w, w_rows_are_outputs):
    x_ref, w_ref = refs[0], refs[1]
    pos = 2
    b_ref = r_ref = None
    if has_bias:
        b_ref = refs[pos]
        pos += 1
    if has_res:
        r_ref = refs[pos]
        pos += 1
    o_ref = refs[pos]

    def finish(acc):
        if has_bias:
            acc = acc + b_ref[...]
        if has_res:
            acc = r_ref[...] + acc
        o_ref[...] = acc.astype(o_ref.dtype)

    if cast_w:
        wb_ref = refs[pos + 1]

        @pl.when(pl.program_id(1) == 0)
        def _():
            wb_ref[...] = w_ref[...].astype(BF16)

        finish((_dot_nt if w_rows_are_outputs else _dot)(x_ref[...], wb_ref[...]))
        return

    part = _dot(x_ref[...], w_ref[...])
    if nk == 1:
        finish(part)
    else:
        acc_ref = refs[pos + 1]
        k = pl.program_id(2)

        @pl.when(k == 0)
        def _():
            acc_ref[...] = part

        @pl.when(k > 0)
        def _():
            acc_ref[...] += part

        @pl.when(k == nk - 1)
        def _():
            finish(acc_ref[...])


def _matmul(x, w, layer, bias=None, residual=None, n_cols=None, out_dtype=F32, tm=832, tn=1024, tk=4096,
            row0=0, rows=None, w_rows_are_outputs=False):
    kd = x.shape[1]
    m = x.shape[0] if rows is None else rows
    n = (w.shape[1] if w_rows_are_outputs else w.shape[2]) if n_cols is None else n_cols
    cast_w = w.dtype != BF16
    assert cast_w or not w_rows_are_outputs
    tm = _pick(math.gcd(m, row0) if row0 else m, tm, 16)
    tn = _pick(n, tn, LANES)
    tk = kd if cast_w else _pick(kd, tk, LANES)
    nk = kd // tk
    rb0 = row0 // tm
    assert residual is None or row0 == 0
    if w_rows_are_outputs:
        w_spec = pl.BlockSpec((None, tn, tk), lambda j, i, k: (layer, j, k))
    else:
        w_spec = pl.BlockSpec((None, tk, tn), lambda j, i, k: (layer, k, j))
    in_specs = [pl.BlockSpec((tm, tk), lambda j, i, k: (rb0 + i, k)), w_spec]
    args = [x, w]
    if bias is not None:
        in_specs.append(pl.BlockSpec((1, tn), lambda j, i, k: (0, j)))
        args.append(bias.reshape(1, n).astype(F32))
    if residual is not None:
        in_specs.append(pl.BlockSpec((tm, tn), lambda j, i, k: (i, j)))
        args.append(residual)
    body = functools.partial(_matmul_body, nk=nk, has_bias=bias is not None, has_res=residual is not None,
                             cast_w=cast_w, w_rows_are_outputs=w_rows_are_outputs)
    if cast_w:
        scratch = [pltpu.VMEM((tn, tk) if w_rows_are_outputs else (tk, tn), BF16)]
    else:
        scratch = [pltpu.VMEM((tm, tn), F32)] if nk > 1 else []
    return pl.pallas_call(
        body,
        grid=(n // tn, m // tm, nk),
        in_specs=in_specs,
        out_specs=pl.BlockSpec((tm, tn), lambda j, i, k: (i, j)),
        out_shape=jax.ShapeDtypeStruct((m, n), out_dtype),
        scratch_shapes=scratch,
        compiler_params=_params("parallel", "arbitrary", "arbitrary"),
        name="matmul",
    )(*args)


def _shift_rows(x, prev8, s):
    if s == 0:
        return x
    xs = pltpu.roll(x, s, 0)
    row = lax.broadcasted_iota(jnp.int32, prev8.shape, 0)
    top = jnp.where(row < s, pltpu.roll(prev8, s, 0), xs[:SUBLANES])
    if x.shape[0] == SUBLANES:
        return top
    return jnp.concatenate([top, xs[SUBLANES:]], axis=0)


def _causal_conv_rows(x, prev8, w):
    kc = w.shape[0]
    y = None
    for kk in range(kc):
        term = _shift_rows(x, prev8, kc - 1 - kk) * w[kk:kk + 1]
        y = term if y is None else y + term
    return y


def _conv_from_state(state_ref, x, w):
    kc = w.shape[0]
    y = state_ref[0] * w[0:1]
    for kk in range(1, kc - 1):
        y = y + state_ref[kk] * w[kk:kk + 1]
    return y + x * w[kc - 1:kc]


def _expand_lanes(x, e):
    hi = x.astype(BF16)
    rest = x - hi.astype(F32)
    mid = rest.astype(BF16)
    lo = (rest - mid.astype(F32)).astype(BF16)
    return _dot(hi, e) + _dot(mid, e) + _dot(lo, e)


def _gdn_gate_body(a_ref, b_ref, alog_ref, dtb_ref, e_ref, *out_refs, chunk, cumulative):
    g = -jnp.exp(alog_ref[...]) * _softplus(a_ref[...] + dtb_ref[...])
    beta = _sigmoid(b_ref[...])
    e = e_ref[...]
    if cumulative:
        r = g.shape[0]
        ri = lax.broadcasted_iota(jnp.int32, (chunk, chunk), 0)
        ci = lax.broadcasted_iota(jnp.int32, (chunk, chunk), 1)
        tril = (ri >= ci).astype(F32)
        gc = jnp.concatenate(
            [_dot(tril, g[c * chunk:(c + 1) * chunk], HIGHEST) for c in range(r // chunk)], axis=0)
        out_refs[0][...] = _expand_lanes(gc, e)
        out_refs[1][...] = _expand_lanes(beta, e)
        out_refs[2][...] = jnp.transpose(gc)
    else:
        out_refs[0][...] = _expand_lanes(g, e)
        out_refs[1][...] = _expand_lanes(beta, e)


def _gdn_gates(ab, alog_pad, dtb_pad, expand, row0, rows, cumulative):
    hd = expand.shape[1]
    tr = _pick(rows, 256, LANES)
    cw = _pick(hd, 1024, LANES)
    rb0 = row0 // tr
    assert row0 % tr == 0 and tr % GDN_CHUNK == 0
    wide = pl.BlockSpec((tr, cw), lambda i, j: (i, j))
    out_specs = [wide, wide]
    out_shape = [jax.ShapeDtypeStruct((rows, hd), F32)] * 2
    if cumulative:
        out_specs.append(pl.BlockSpec((LANES, tr), lambda i, j: (0, i)))
        out_shape.append(jax.ShapeDtypeStruct((LANES, rows), F32))
    body = functools.partial(_gdn_gate_body, chunk=GDN_CHUNK, cumulative=cumulative)
    return pl.pallas_call(
        body,
        grid=(rows // tr, hd // cw),
        in_specs=[
            pl.BlockSpec((tr, LANES), lambda i, j: (rb0 + i, 0)),
            pl.BlockSpec((tr, LANES), lambda i, j: (rb0 + i, 1)),
            pl.BlockSpec((1, LANES), lambda i, j: (0, 0)),
            pl.BlockSpec((1, LANES), lambda i, j: (0, 0)),
            pl.BlockSpec((LANES, cw), lambda i, j: (0, j)),
        ],
        out_specs=out_specs,
        out_shape=out_shape,
        compiler_params=_params("parallel", "arbitrary"),
        name="gdn_gates",
    )(ab, ab, alog_pad, dtb_pad, expand)


def _gated_head_norm(o, z, nw):
    o = o * lax.rsqrt(jnp.mean(o * o, axis=-1, keepdims=True) + NORM_EPS) * nw
    return o * _silu(z)


def _gdn_block_masks(r, chunk, dk):
    ri = lax.broadcasted_iota(jnp.int32, (r, r), 0)
    ci = lax.broadcasted_iota(jnp.int32, (r, r), 1)
    first = (ri // chunk) * chunk
    causal = (ci >= first) & (ci <= ri)
    strict = (ci >= first) & (ci < ri)
    eye = (ri == ci).astype(F32)
    bi = lax.broadcasted_iota(jnp.int32, (r // chunk * dk, r), 0) // dk
    bj = lax.broadcasted_iota(jnp.int32, (r // chunk * dk, r), 1) // chunk
    return causal, strict, eye, bi == bj


def _gdn_block_terms(qs, ks, vs, gcbs, gc_rows, bebs, chunk, masks):
    nh = len(qs)
    hs = range(nh)
    r = ks[0].shape[0]
    dv = vs[0].shape[1]
    nch = r // chunk
    causal, strict, eye, kt_mask = masks
    wide = lambda g: g[:, :r] if r <= LANES else jnp.concatenate([g] * (r // LANES), axis=1)
    decay = [jnp.where(causal, jnp.exp(jnp.where(causal, wide(gcbs[h]) - gc_rows[h], 0.0)), 0.0) for h in hs]
    eg = [jnp.exp(gcbs[h]) for h in hs]
    kb = [ks[h] * bebs[h] for h in hs]
    kq = [_dot_nt(jnp.concatenate([kb[h], qs[h]], axis=0).astype(BF16), ks[h].astype(BF16)) for h in hs]
    qk = [kq[h][r:] * decay[h] for h in hs]
    n = [-jnp.where(strict, kq[h][:r] * decay[h], 0.0) for h in hs]
    tm = [eye + n[h] for h in hs]
    nb = [n[h].astype(BF16) for h in hs]
    p = [_dot(nb[h], nb[h]) for h in hs]
    steps = int(math.log2(chunk)) - 1
    for i in range(steps):
        pb = [p[h].astype(BF16) for h in hs]
        if i < steps - 1:
            both = [_dot(jnp.concatenate([pb[h], tm[h].astype(BF16)], axis=0), pb[h]) for h in hs]
            p = [both[h][:r] for h in hs]
            tm = [tm[h] + both[h][r:] for h in hs]
        else:
            tm = [tm[h] + _dot(tm[h].astype(BF16), pb[h]) for h in hs]
    uwb = [_dot(tm[h].astype(BF16), jnp.concatenate([vs[h] * bebs[h], kb[h] * eg[h]], axis=1).astype(BF16)
                ).astype(BF16) for h in hs]
    gl = [jnp.concatenate(
        [jnp.broadcast_to(gcbs[h][(c + 1) * chunk - 1:(c + 1) * chunk, :], (chunk, LANES)) for c in range(nch)],
        axis=0) for h in hs]
    k_tail_t = [jnp.transpose(ks[h] * jnp.exp(gl[h] - gcbs[h])) for h in hs]
    kt_blocks = [jnp.where(kt_mask, jnp.concatenate([k_tail_t[h]] * nch, axis=0), 0.0) for h in hs]
    both = [_dot(jnp.concatenate([qk[h], kt_blocks[h]], axis=0).astype(BF16), uwb[h]) for h in hs]
    return [(qs[h] * eg[h] - both[h][:r, dv:], both[h][:r, :dv], both[h][r:, dv:], both[h][r:, :dv], gl[h])
            for h in hs]


def _gdn_prompt_body(q_ref, k_ref, v_ref, z_ref, gc_ref, gr_ref, be_ref, wq_ref, wk_ref, wv_ref, nw_ref,
                     o_init_ref, o_ref, s_out_ref, s_scr, carry_scr, *, heads, chunk, block_rows, q_scale):
    del o_init_ref
    t = pl.program_id(2)

    @pl.when(t == 0)
    def _():
        s_scr[...] = jnp.zeros_like(s_scr)
        carry_scr[...] = jnp.zeros_like(carry_scr)

    r = q_ref.shape[0]

    def conv_silu(x_ref, w_ref, idx):
        x = x_ref[...]
        y = _causal_conv_rows(x, carry_scr[idx], w_ref[...])
        carry_scr[idx] = x[r - SUBLANES:]
        return _silu(y)

    qc = conv_silu(q_ref, wq_ref, 0)
    kc = conv_silu(k_ref, wk_ref, 1)
    vc = conv_silu(v_ref, wv_ref, 2)

    nw = nw_ref[...]
    dk = LANES
    br = block_rows
    cpb = br // chunk
    masks = _gdn_block_masks(br, chunk, dk)

    lanes = [slice(h * LANES, (h + 1) * LANES) for h in range(heads)]
    qs, ks = [], []
    for ls in lanes:
        qh = qc[:, ls]
        kh = kc[:, ls]
        qs.append(qh * lax.rsqrt(jnp.sum(qh * qh, axis=-1, keepdims=True) + L2_EPS) * q_scale)
        ks.append(kh * lax.rsqrt(jnp.sum(kh * kh, axis=-1, keepdims=True) + L2_EPS))
    items = [(b, h) for b in range(r // br) for h in range(heads)]
    rows_of = lambda b: slice(b * br, (b + 1) * br)
    terms = _gdn_block_terms(
        [qs[h][rows_of(b)] for b, h in items], [ks[h][rows_of(b)] for b, h in items],
        [vc[rows_of(b), lanes[h]] for b, h in items], [gc_ref[rows_of(b), lanes[h]] for b, h in items],
        [gr_ref[h, 0][:, rows_of(b)] for b, h in items], [be_ref[rows_of(b), lanes[h]] for b, h in items],
        chunk, masks)
    s = [s_scr[h] for h in range(heads)]
    for c in range(r // chunk):
        b, cl = divmod(c, cpb)
        rs = slice(c * chunk, (c + 1) * chunk)
        rl = slice(cl * chunk, (cl + 1) * chunk)
        bs = slice(cl * dk, (cl + 1) * dk)
        for h, ls in enumerate(lanes):
            q_eff, o0, m_mat, b_mat, gl = terms[b * heads + h]
            lhs = jnp.concatenate([q_eff[rl], m_mat[bs]], axis=0).astype(BF16)
            ls_s = _dot(lhs, s[h].astype(BF16))
            o = ls_s[:chunk] + o0[rl]
            s_decay = jnp.broadcast_to(jnp.exp(gl[cl * chunk:cl * chunk + 1, :]), s[h].shape)
            s[h] = s[h] * s_decay - ls_s[chunk:] + b_mat[bs]
            o_ref[rs, ls] = _gated_head_norm(o, z_ref[rs, ls], nw).astype(o_ref.dtype)
    for h in range(heads):
        s_scr[h] = s[h]
        s_out_ref[0, h] = s[h]


def _gdn_prompt(proj, gcb, gc_t, beb, conv_w, norm_w, m_total, batch, seq, n_heads, rows_per_step,
                heads_per_step):
    hd = n_heads * LANES
    g = heads_per_step
    gw = g * LANES
    r = rows_per_step
    nt = seq // r
    ncb = hd // gw
    assert r % LANES == 0 and r % GDN_CHUNK == 0
    gc_rows = gc_t.reshape(LANES, (batch * seq) // r, 1, r)
    row = lambda b, hg, t: b * nt + t
    sect = lambda s: pl.BlockSpec((r, gw), lambda b, hg, t: (row(b, hg, t), s * ncb + hg))
    gate = pl.BlockSpec((r, gw), lambda b, hg, t: (row(b, hg, t), hg))
    wsp = lambda s: pl.BlockSpec((conv_w.shape[0], gw), lambda b, hg, t: (0, s * ncb + hg))
    body = functools.partial(_gdn_prompt_body, heads=g, chunk=GDN_CHUNK, block_rows=LANES,
                             q_scale=float(LANES) ** -0.5)
    return pl.pallas_call(
        body,
        grid=(batch, n_heads // g, nt),
        in_specs=[sect(0), sect(1), sect(2), sect(3), gate,
                  pl.BlockSpec((g, 1, 1, r), lambda b, hg, t: (hg, row(b, hg, t), 0, 0)),
                  gate, wsp(0), wsp(1), wsp(2),
                  pl.BlockSpec((1, LANES), lambda b, hg, t: (0, 0)),
                  pl.BlockSpec(memory_space=pl.ANY)],
        out_specs=[
            pl.BlockSpec((r, gw), lambda b, hg, t: (row(b, hg, t), hg)),
            pl.BlockSpec((1, g, LANES, LANES), lambda b, hg, t: (b, hg, 0, 0)),
        ],
        out_shape=[
            jax.ShapeDtypeStruct((m_total, hd), BF16),
            jax.ShapeDtypeStruct((batch, n_heads, LANES, LANES), F32),
        ],
        input_output_aliases={11: 0},
        scratch_shapes=[pltpu.VMEM((g, LANES, LANES), F32), pltpu.VMEM((3, SUBLANES, gw), F32)],
        compiler_params=_params("parallel", "parallel", "arbitrary"),
        name="gdn_prompt",
    )(proj, proj, proj, proj, gcb, gc_rows, beb, conv_w, conv_w, conv_w, norm_w.reshape(1, LANES),
      jnp.zeros((m_total, hd), BF16))


def _gdn_sample_body(q_ref, k_ref, v_ref, z_ref, sq_ref, sk_ref, sv_ref, g_ref, be_ref, wq_ref, wk_ref, wv_ref,
                     nw_ref, s_ref, *rest, q_scale):
    o_ref, s_out_ref, q_scr, k_scr, d_scr, o_scr = rest[-6:]
    nb = q_ref.shape[0]
    q = _silu(_conv_from_state(sq_ref, q_ref[...], wq_ref[...]))
    k = _silu(_conv_from_state(sk_ref, k_ref[...], wk_ref[...]))
    v = _silu(_conv_from_state(sv_ref, v_ref[...], wv_ref[...]))
    q_scr[...] = q * lax.rsqrt(jnp.sum(q * q, axis=-1, keepdims=True) + L2_EPS) * q_scale
    k_scr[...] = k * lax.rsqrt(jnp.sum(k * k, axis=-1, keepdims=True) + L2_EPS)
    d_scr[0] = jnp.exp(g_ref[...])
    d_scr[1] = be_ref[...]
    d_scr[2] = v

    def step(b, carry):
        row = pl.ds(b, 1)
        k_col = jnp.transpose(jnp.broadcast_to(k_scr[row, :], (LANES, LANES)))
        q_col = jnp.transpose(jnp.broadcast_to(q_scr[row, :], (LANES, LANES)))
        s = s_ref[b] * d_scr[0, row, :]
        ks = jnp.sum(k_col * s, axis=0, keepdims=True)
        delta = d_scr[1, row, :] * (d_scr[2, row, :] - ks)
        s = s + k_col * delta
        s_out_ref[b] = s
        o_scr[row, :] = jnp.sum(q_col * s, axis=0, keepdims=True)
        return carry

    lax.fori_loop(0, nb, step, 0, unroll=4)
    o_ref[...] = _gated_head_norm(o_scr[...], z_ref[...], nw_ref[...]).astype(o_ref.dtype)


def _gdn_sample(proj, conv_state, gb, beb, conv_w, norm_w, s_all, layer, o_full, s_new_all, row0, n_heads,
                samples_per_step):
    nsamp = s_all.shape[1]
    bs = samples_per_step
    assert row0 % bs == 0 and nsamp % bs == 0
    rb0 = row0 // bs
    nh = n_heads
    kc = conv_w.shape[0]
    sect = lambda s: pl.BlockSpec((bs, LANES), lambda h, i: (rb0 + i, s * nh + h))
    stat = lambda s: pl.BlockSpec((kc - 1, bs, LANES), lambda h, i: (0, i, s * nh + h))
    gate = pl.BlockSpec((bs, LANES), lambda h, i: (i, h))
    wsp = lambda s: pl.BlockSpec((kc, LANES), lambda h, i: (0, s * nh + h))
    sspec = pl.BlockSpec((None, bs, None, LANES, LANES), lambda h, i: (layer, i, h, 0, 0))
    untouched = pl.BlockSpec(memory_space=pl.ANY)
    in_specs = [sect(0), sect(1), sect(2), sect(3), stat(0), stat(1), stat(2), gate, gate,
                wsp(0), wsp(1), wsp(2), pl.BlockSpec((1, LANES), lambda h, i: (0, 0)), sspec, untouched]
    args = [proj, proj, proj, proj, conv_state, conv_state, conv_state, gb, beb, conv_w, conv_w, conv_w,
            norm_w.reshape(1, LANES), s_all, o_full]
    aliases = {len(args) - 1: 0}
    if s_new_all is not None:
        in_specs.append(untouched)
        args.append(s_new_all)
        aliases[len(args) - 1] = 1
    body = functools.partial(_gdn_sample_body, q_scale=float(LANES) ** -0.5)
    return pl.pallas_call(
        body,
        grid=(nh, nsamp // bs),
        in_specs=in_specs,
        out_specs=[pl.BlockSpec((bs, LANES), lambda h, i: (rb0 + i, h)), sspec],
        out_shape=[
            jax.ShapeDtypeStruct(o_full.shape, o_full.dtype),
            jax.ShapeDtypeStruct(s_all.shape, F32),
        ],
        input_output_aliases=aliases,
        scratch_shapes=[pltpu.VMEM((bs, LANES), F32), pltpu.VMEM((bs, LANES), F32),
                        pltpu.VMEM((3, bs, LANES), F32), pltpu.VMEM((bs, LANES), F32)],
        compiler_params=_params("parallel", "parallel"),
        name="gdn_sample",
    )(*args)


def _lru_coeffs(xc, wg_ref, bgr_ref, bgi_ref, lam_ref):
    bw = xc.shape[1]
    gates = _dot(xc.astype(BF16), wg_ref[0])
    r = _sigmoid(gates[:, :bw] + bgr_ref[...])
    i = _sigmoid(gates[:, bw:] + bgi_ref[...])
    log_a = -LRU_C * r * _softplus(-lam_ref[...])
    a = jnp.exp(log_a)
    mult = jnp.sqrt(-jnp.tanh(log_a) * (a * a + 1.0))
    return a, mult * (i * xc)


def _scan8(a, b):
    row = lax.broadcasted_iota(jnp.int32, a.shape, 0)
    for s in (1, 2, 4):
        a_sh = pltpu.roll(a, s, 0)
        b_sh = pltpu.roll(b, s, 0)
        m = row >= s
        b = jnp.where(m, a * b_sh + b, b)
        a = jnp.where(m, a * a_sh, a)
    return a, b


def _lru_prompt_body(x_ref, y_ref, cw_ref, cb_ref, wg_ref, bgr_ref, bgi_ref, lam_ref, o_init_ref,
                     o_ref, h_out_ref, carry_scr, h_scr, a_scr, b_scr):
    del o_init_ref
    t = pl.program_id(2)

    @pl.when(t == 0)
    def _():
        carry_scr[...] = jnp.zeros_like(carry_scr)
        h_scr[...] = jnp.zeros_like(h_scr)

    x = x_ref[...]
    r = x.shape[0]
    xc = _causal_conv_rows(x, carry_scr[...], cw_ref[...]) + cb_ref[...]
    carry_scr[...] = x[r - SUBLANES:]
    a, b = _lru_coeffs(xc, wg_ref, bgr_ref, bgi_ref, lam_ref)
    a_scr[...] = a
    b_scr[...] = b

    def group(i, h_prev):
        rows = pl.ds(pl.multiple_of(i * SUBLANES, SUBLANES), SUBLANES)
        a8, b8 = _scan8(a_scr[rows, :], b_scr[rows, :])
        h8 = a8 * h_prev + b8
        b_scr[rows, :] = h8
        return jnp.broadcast_to(h8[SUBLANES - 1:SUBLANES, :], h8.shape)

    h_last = lax.fori_loop(0, r // SUBLANES, group, h_scr[...], unroll=8)
    h_scr[...] = h_last
    h_out_ref[0] = h_last[0:1, :]
    o_ref[...] = (b_scr[...] * _gelu_tanh(y_ref[...])).astype(o_ref.dtype)


def _lru_prompt(proj, conv_w, conv_b, w_gate, b_gate, lam, m_total, batch, seq, rows_per_step):
    nb, bw = w_gate.shape[0], w_gate.shape[1]
    drnn = nb * bw
    r = rows_per_step
    nt = seq // r
    row = lambda b, n, t: b * nt + t
    vec = lambda off: pl.BlockSpec((1, bw), lambda b, n, t: (0, off + n))
    return pl.pallas_call(
        _lru_prompt_body,
        grid=(batch, nb, nt),
        in_specs=[
            pl.BlockSpec((r, bw), lambda b, n, t: (row(b, n, t), n)),
            pl.BlockSpec((r, bw), lambda b, n, t: (row(b, n, t), nb + n)),
            pl.BlockSpec((conv_w.shape[0], bw), lambda b, n, t: (0, n)),
            vec(0),
            pl.BlockSpec((1, bw, 2 * bw), lambda b, n, t: (n, 0, 0)),
            vec(0), vec(nb), vec(0),
            pl.BlockSpec(memory_space=pl.ANY),
        ],
        out_specs=[
            pl.BlockSpec((r, bw), lambda b, n, t: (row(b, n, t), n)),
            pl.BlockSpec((1, 1, bw), lambda b, n, t: (b, 0, n)),
        ],
        out_shape=[
            jax.ShapeDtypeStruct((m_total, drnn), BF16),
            jax.ShapeDtypeStruct((batch, 1, drnn), F32),
        ],
        input_output_aliases={8: 0},
        scratch_shapes=[pltpu.VMEM((SUBLANES, bw), F32), pltpu.VMEM((SUBLANES, bw), F32),
                        pltpu.VMEM((r, bw), F32), pltpu.VMEM((r, bw), F32)],
        compiler_params=_params("parallel", "parallel", "arbitrary"),
        name="lru_prompt",
    )(proj, proj, conv_w, conv_b.reshape(1, drnn), w_gate, b_gate.reshape(1, 2 * drnn),
      b_gate.reshape(1, 2 * drnn), lam.reshape(1, drnn), jnp.zeros((m_total, drnn), BF16))


def _lru_sample_body(x_ref, y_ref, st_ref, h0_ref, cw_ref, cb_ref, wg_ref, bgr_ref, bgi_ref, lam_ref,
                     o_full_ref, o_ref, h_out_ref):
    del o_full_ref
    xc = _conv_from_state(st_ref, x_ref[...], cw_ref[...]) + cb_ref[...]
    a, b = _lru_coeffs(xc, wg_ref, bgr_ref, bgi_ref, lam_ref)
    h = a * h0_ref[...] + b
    h_out_ref[...] = h
    o_ref[...] = (h * _gelu_tanh(y_ref[...])).astype(o_ref.dtype)


def _lru_sample(proj, conv_state, h0, conv_w, conv_b, w_gate, b_gate, lam, o_full, row0):
    nsamp = h0.shape[0]
    nb, bw = w_gate.shape[0], w_gate.shape[1]
    drnn = nb * bw
    kc = conv_w.shape[0]
    assert row0 % nsamp == 0
    rb0 = row0 // nsamp
    vec = lambda off: pl.BlockSpec((1, bw), lambda n: (0, off + n))
    return pl.pallas_call(
        _lru_sample_body,
        grid=(nb,),
        in_specs=[
            pl.BlockSpec((nsamp, bw), lambda n: (rb0, n)),
            pl.BlockSpec((nsamp, bw), lambda n: (rb0, nb + n)),
            pl.BlockSpec((kc - 1, nsamp, bw), lambda n: (0, 0, n)),
            pl.BlockSpec((nsamp, bw), lambda n: (0, n)),
            pl.BlockSpec((kc, bw), lambda n: (0, n)),
            vec(0),
            pl.BlockSpec((1, bw, 2 * bw), lambda n: (n, 0, 0)),
            vec(0), vec(nb), vec(0),
            pl.BlockSpec(memory_space=pl.ANY),
        ],
        out_specs=[pl.BlockSpec((nsamp, bw), lambda n: (rb0, n)), pl.BlockSpec((nsamp, bw), lambda n: (0, n))],
        out_shape=[jax.ShapeDtypeStruct(o_full.shape, o_full.dtype), jax.ShapeDtypeStruct((nsamp, drnn), F32)],
        input_output_aliases={10: 0},
        compiler_params=_params("parallel"),
        name="lru_sample",
    )(proj, proj, conv_state, h0, conv_w, conv_b.reshape(1, drnn), w_gate, b_gate.reshape(1, 2 * drnn),
      b_gate.reshape(1, 2 * drnn), lam.reshape(1, drnn), o_full)


def _ffn_body(x_ref, wg32_ref, wv32_ref, cw_ref, cb_ref, st_ref, act_ref, tail_ref, gs_ref, carry_scr, wg_ref,
              wv_ref, *, tiles_per_seq, n_sub):
    i = pl.program_id(1)
    nsamp = gs_ref.shape[0]
    cw = cw_ref[...]
    cb = cb_ref[...]

    @pl.when(i == 0)
    def _():
        wg_ref[...] = wg32_ref[...].astype(BF16)
        wv_ref[...] = wv32_ref[...].astype(BF16)
        xs = x_ref[:nsamp, :]
        g = _dot(xs, wg_ref[...])
        v = _dot(xs, wv_ref[...])
        gate = _conv_from_state(st_ref, g, cw) + cb
        act_ref[:nsamp, :] = (_gelu_erf(gate) * v).astype(act_ref.dtype)
        gs_ref[...] = g
        tail_ref[...] = jnp.zeros_like(tail_ref)

    @pl.when((i - 1) % tiles_per_seq == 0)
    def _():
        carry_scr[...] = jnp.zeros_like(carry_scr)

    @pl.when(i > 0)
    def _():
        sub = x_ref.shape[0] // n_sub
        rows = [slice(s * sub, (s + 1) * sub) for s in range(n_sub)]

        def project(s):
            xs = x_ref[rows[s], :]
            return _dot(xs, wg_ref[...]), _dot(xs, wv_ref[...])

        prev8 = carry_scr[...]
        g, v = project(0)
        for s in range(n_sub):
            nxt = project(s + 1) if s + 1 < n_sub else None
            gate = _causal_conv_rows(g, prev8, cw) + cb
            act_ref[rows[s], :] = (_gelu_erf(gate) * v).astype(act_ref.dtype)
            prev8 = g[sub - SUBLANES:]
            if nxt is not None:
                g, v = nxt
        carry_scr[...] = prev8
        tail_ref[...] = prev8


def _ffn_in(h, w_in, layer, conv_w, conv_b, conv_state, batch, seq):
    m_total, kd = h.shape
    dff = conv_w.shape[1]
    kc = conv_w.shape[0]
    mp = batch * seq
    nsamp = m_total - mp
    tm = _pick(seq, 1024, 16)
    tn = _pick(dff, 256, LANES)
    nj, ni = dff // tn, mp // tm
    assert nsamp <= tm and nsamp % SUBLANES == 0
    n_sub = 4 if tm % (4 * SUBLANES * 2) == 0 else 1
    body = functools.partial(_ffn_body, tiles_per_seq=seq // tm, n_sub=n_sub)
    tile = lambda i: (i + ni) % (ni + 1)
    return pl.pallas_call(
        body,
        grid=(nj, ni + 1),
        in_specs=[
            pl.BlockSpec((tm, kd), lambda j, i: (tile(i), 0)),
            pl.BlockSpec((None, kd, tn), lambda j, i: (layer, 0, j)),
            pl.BlockSpec((None, kd, tn), lambda j, i: (layer, 0, nj + j)),
            pl.BlockSpec((kc, tn), lambda j, i: (0, j)),
            pl.BlockSpec((1, tn), lambda j, i: (0, j)),
            pl.BlockSpec((kc - 1, nsamp, tn), lambda j, i: (0, 0, j)),
        ],
        out_specs=[
            pl.BlockSpec((tm, tn), lambda j, i: (tile(i), j)),
            pl.BlockSpec((SUBLANES, tn), lambda j, i: (tile(i), j)),
            pl.BlockSpec((nsamp, tn), lambda j, i: (0, j)),
        ],
        out_shape=[
            jax.ShapeDtypeStruct((m_total, dff), BF16),
            jax.ShapeDtypeStruct(((ni + 1) * SUBLANES, dff), F32),
            jax.ShapeDtypeStruct((nsamp, dff), F32),
        ],
        scratch_shapes=[pltpu.VMEM((SUBLANES, tn), F32), pltpu.VMEM((kd, tn), BF16), pltpu.VMEM((kd, tn), BF16)],
        compiler_params=_params("parallel", "arbitrary"),
        name="ffn_in",
    )(h, w_in, w_in, conv_w, conv_b.reshape(1, dff), conv_state)


def _new_conv_state(old_state, pre_rows):
    return jnp.concatenate([old_state[:, 1:], pre_rows[:, None, :]], axis=1)


def _prompt_conv_tail(proj, batch, seq, keep, cols):
    return jnp.stack([lax.slice(proj, ((b + 1) * seq - keep, 0), ((b + 1) * seq, cols)) for b in range(batch)])


def _sample_rows(proj, mp, cols):
    return lax.slice(proj, (mp, 0), (proj.shape[0], cols))


def _gdn_gate_weights(w_in_t, n_heads, first_row):
    n_layers, d_model = w_in_t.shape[0], w_in_t.shape[2]
    tail = lax.slice(w_in_t, (0, first_row, 0), (n_layers, first_row + 2 * n_heads, d_model))
    zeros = jnp.zeros((n_layers, LANES - n_heads, d_model), w_in_t.dtype)
    return jnp.concatenate([tail[:, :n_heads], zeros, tail[:, n_heads:], zeros], axis=1)


def _gdn_layer(x, h, mp, batch, seq, j, s_all, s_new_all, conv_state, w_in_t, w_ab_t, conv_w, a_log, dt_bias,
               norm_w, w_out):
    n_heads = a_log.shape[0]
    m_total = x.shape[0]
    vd = w_out.shape[1]
    cd = conv_w.shape[1]
    nsamp = s_all.shape[1]
    assert vd == n_heads * LANES and cd == 3 * vd, "kernels assume DK == DV == 128"
    kc = conv_w.shape[0]

    pad = lambda p: jnp.zeros((1, LANES), F32).at[0, :n_heads].set(p.astype(F32))
    expand = (jnp.arange(LANES)[:, None] == (jnp.arange(vd) // LANES)[None, :]).astype(BF16)

    proj = _matmul(h, w_in_t, j, n_cols=cd + vd, tm=416, tn=1024, w_rows_are_outputs=True)
    ab = _matmul(h, w_ab_t, j, w_rows_are_outputs=True)

    gcb, beb, gc_t = _gdn_gates(ab, pad(a_log), pad(dt_bias), expand, 0, mp, True)
    o, s_p = _gdn_prompt(proj, gcb, gc_t, beb, conv_w, norm_w, m_total, batch, seq, n_heads,
                         rows_per_step=min(seq, 256), heads_per_step=min(n_heads, 4))

    gb_s, beb_s = _gdn_gates(ab, pad(a_log), pad(dt_bias), expand, mp, nsamp, False)
    state_t = jnp.transpose(conv_state, (1, 0, 2))
    o, s_new_all = _gdn_sample(proj, state_t, gb_s, beb_s, conv_w, norm_w, s_all, j, o, s_new_all, mp,
                               n_heads, samples_per_step=min(nsamp, 64))

    x = _matmul(o, w_out, j, residual=x, tn=512)

    conv_p = _prompt_conv_tail(proj, batch, seq, kc - 1, cd)
    conv_s = _new_conv_state(conv_state, _sample_rows(proj, mp, cd))
    return x, s_p, s_new_all, conv_p, conv_s


def _lru_layer(x, h, mp, batch, seq, j, h0, conv_state, w_in, b_in, conv_w, conv_b, w_gate_bf, b_gate, lam,
               w_out):
    m_total = x.shape[0]
    drnn = conv_w.shape[1]
    kc = conv_w.shape[0]
    proj = _matmul(h, w_in, j, bias=b_in, tm=416, tn=1024)
    o, h_p = _lru_prompt(proj, conv_w, conv_b, w_gate_bf, b_gate, lam, m_total, batch, seq,
                         rows_per_step=min(seq, 512))
    state_t = jnp.transpose(conv_state, (1, 0, 2))
    o, h_s = _lru_sample(proj, state_t, h0, conv_w, conv_b, w_gate_bf, b_gate, lam, o, mp)
    x = _matmul(o, w_out, j, residual=x, tn=512)
    conv_p = _prompt_conv_tail(proj, batch, seq, kc - 1, drnn)
    conv_s = _new_conv_state(conv_state, _sample_rows(proj, mp, drnn))
    return x, h_p.reshape(batch, drnn), h_s, conv_p, conv_s


def _ffn_layer(x, h, mp, batch, seq, i, conv_state, w_in, conv_w, conv_b, w_out):
    m_total = x.shape[0]
    dff = conv_w.shape[1]
    kc = conv_w.shape[0]
    state_t = jnp.transpose(conv_state, (1, 0, 2))
    act, tails, gate_s = _ffn_in(h, w_in, i, conv_w, conv_b, state_t, batch, seq)
    x = _matmul(act, w_out, i, residual=x, tm=416, tn=512, tk=dff)
    n_tiles = tails.shape[0] // SUBLANES - 1
    tails = tails[:n_tiles * SUBLANES].reshape(batch, -1, SUBLANES, dff)
    conv_p = tails[:, -1, SUBLANES - (kc - 1):, :]
    conv_s = _new_conv_state(conv_state, gate_s)
    return x, conv_p, conv_s


def kernel(x_prompt, x_sample, state_gdn_S, state_gdn_conv, state_lru_h, state_lru_conv, state_ffn_conv, norm_mixer, norm_ffn, norm_final, gdn_w_in, gdn_conv_w, gdn_A_log, gdn_dt_bias, gdn_norm_w, gdn_w_out, lru_w_in, lru_b_in, lru_conv_w, lru_conv_b, lru_w_gate, lru_b_gate, lru_lambda, lru_w_out, ffn_w_in, ffn_conv_w, ffn_conv_b, ffn_w_out):
    batch, seq, d_model = x_prompt.shape
    nsamp = x_sample.shape[0]
    depth = norm_mixer.shape[0]
    mp = batch * seq
    x = jnp.concatenate([x_prompt.reshape(mp, d_model), x_sample.reshape(nsamp, d_model)], axis=0)

    n_heads = gdn_A_log.shape[1]
    gdn_w_in_t = jnp.swapaxes(gdn_w_in, 1, 2)
    gdn_w_ab_t = _gdn_gate_weights(gdn_w_in_t, n_heads, gdn_w_in.shape[2] - 2 * n_heads)
    lru_w_gate_bf = lru_w_gate.astype(BF16)
    ffn_w_out = ffn_w_out.astype(BF16)

    gs_p, gc_p, gc_s = [], [], []
    gs_s = None
    lh_p, lh_s, lc_p, lc_s = [], [], [], []
    fc_p, fc_s = [], []
    for i in range(depth):
        j = i // 2
        h = _rmsnorm(x, norm_mixer[i], BF16)
        if i % 2 == 0:
            x, sp, gs_s, cp, cs = _gdn_layer(
                x, h, mp, batch, seq, j, state_gdn_S, gs_s, state_gdn_conv[j], gdn_w_in_t, gdn_w_ab_t,
                gdn_conv_w[j], gdn_A_log[j], gdn_dt_bias[j], gdn_norm_w[j], gdn_w_out)
            gs_p.append(sp); gc_p.append(cp); gc_s.append(cs)
        else:
            x, hp, hs, cp, cs = _lru_layer(
                x, h, mp, batch, seq, j, state_lru_h[j], state_lru_conv[j], lru_w_in, lru_b_in[j],
                lru_conv_w[j], lru_conv_b[j], lru_w_gate_bf[j], lru_b_gate[j], lru_lambda[j], lru_w_out)
            lh_p.append(hp); lh_s.append(hs); lc_p.append(cp); lc_s.append(cs)
        h = _rmsnorm(x, norm_ffn[i], BF16)
        x, cp, cs = _ffn_layer(x, h, mp, batch, seq, i, state_ffn_conv[i], ffn_w_in, ffn_conv_w[i],
                               ffn_conv_b[i], ffn_w_out)
        fc_p.append(cp); fc_s.append(cs)

    y_prompt = _rmsnorm(x, norm_final, F32, 0, mp).reshape(batch, seq, d_model)
    y_sample = _rmsnorm(x, norm_final, F32, mp, nsamp).reshape(nsamp, 1, d_model)
    return (y_prompt, y_sample,
            jnp.stack(gs_p), gs_s, jnp.stack(gc_p), jnp.stack(gc_s),
            jnp.stack(lh_p), jnp.stack(lh_s), jnp.stack(lc_p), jnp.stack(lc_s),
            jnp.stack(fc_p), jnp.stack(fc_s))
```
